```python
import jax, jax.numpy as jnp
from jax import lax
import numpy as np

D_MODEL = 1024
BATCH = 2
SEQ = 8192
DEPTH = 1

CHUNK = 64
EPS = 1e-6
GLA_HEADS = 4
GLA_DK = D_MODEL // 8
GLA_DV = D_MODEL // 4
GLA_GATE_RANK = 16
GLA_TAU = 16.0
RET_HEADS = 4
RET_DK = D_MODEL // 4
RET_DV = D_MODEL // 2
ROPE_BASE = 10000.0
D_FF = 2816
CONV_W = 3

GLA_QK = GLA_HEADS * GLA_DK
GLA_V = GLA_HEADS * GLA_DV
RET_QK = RET_HEADS * RET_DK
RET_V = RET_HEADS * RET_DV
IN_SPLITS = (GLA_QK, GLA_QK, GLA_V, GLA_V, GLA_GATE_RANK, RET_QK, RET_QK, RET_V, RET_V, D_MODEL, D_MODEL)
IN_COLS = GLA_QK * 2 + GLA_V * 2 + GLA_GATE_RANK + RET_QK * 2 + RET_V * 2 + D_MODEL * 2

kernel_name = "hybrid_gla_retnet_convffn_block"


def rmsnorm(x, w):
    xf = x.astype(jnp.float32)
    y = xf * lax.rsqrt(jnp.mean(xf * xf, axis=-1, keepdims=True) + EPS)
    return (y * w.astype(jnp.float32)).astype(x.dtype)


def head_layernorm(o):
    mu = jnp.mean(o, axis=-1, keepdims=True)
    var = jnp.mean(jnp.square(o - mu), axis=-1, keepdims=True)
    return (o - mu) * lax.rsqrt(var + EPS)


def to_chunks(t, heads):
    b, s, _ = t.shape
    t = t.reshape(b, s // CHUNK, CHUNK, heads, -1)
    return t.transpose(1, 0, 3, 2, 4)


def from_chunks(t):
    n, b, h, c, d = t.shape
    return t.transpose(1, 0, 3, 2, 4).reshape(b, n * c, h * d)


def rope(x, positions):
    d = x.shape[-1]
    inv_freq = ROPE_BASE ** (-jnp.arange(0, d, 2, dtype=jnp.float32) / d)
    ang = positions.astype(jnp.float32)[..., None] * inv_freq
    cos = jnp.cos(ang)[:, :, None, :]
    sin = jnp.sin(ang)[:, :, None, :]
    x1, x2 = x[..., : d // 2], x[..., d // 2:]
    return jnp.concatenate([x1 * cos - x2 * sin, x1 * sin + x2 * cos], axis=-1)


def gla_mixer(q, k, v, log_a):
    b = q.shape[0]
    qc = to_chunks(q * (GLA_DK ** -0.5), GLA_HEADS)
    kc = to_chunks(k, GLA_HEADS)
    vc = to_chunks(v, GLA_HEADS)
    gc = to_chunks(log_a, GLA_HEADS)

    def step(state, inp):
        qi, ki, vi, gi = inp
        G = jnp.cumsum(gi, axis=2)
        decay = jnp.exp(-jnp.abs(G[:, :, :, None, :] - G[:, :, None, :, :]))
        scores = jnp.einsum('bhnd,bhmd,bhnmd->bhnm', qi, ki, decay)
        o = jnp.einsum('bhnm,bhme->bhne', scores, vi) + \
            jnp.einsum('bhnd,bhde->bhne', qi * jnp.exp(G), state)
        G_last = G[:, :, -1:, :]
        k_dec = ki * jnp.exp(G_last - G)
        state = jnp.exp(G_last[:, :, 0, :])[..., None] * state + \
            jnp.einsum('bhmd,bhme->bhde', k_dec, vi)
        return state, o

    s0 = jnp.zeros((b, GLA_HEADS, GLA_DK, GLA_DV), jnp.float32)
    _, oc = lax.scan(step, s0, (qc, kc, vc, gc))
    return oc


def retention_mixer(q, k, v):
    b, s = q.shape[0], q.shape[1]
    qc = to_chunks(q.reshape(b, s, -1), RET_HEADS)
    kc = to_chunks(k.reshape(b, s, -1) * (RET_DK ** -0.5), RET_HEADS)
    vc = to_chunks(v, RET_HEADS)
    log_gamma = jnp.log(1.0 - jnp.exp2(-5.0 - jnp.arange(RET_HEADS, dtype=jnp.float32)))
    idx = jnp.arange(CHUNK, dtype=jnp.float32)
    intra_decay = jnp.exp(log_gamma[:, None, None] * jnp.abs(idx[:, None] - idx[None, :]))
    q_decay = jnp.exp(log_gamma[:, None] * (idx + 1.0))[..., None]
    k_decay = jnp.exp(log_gamma[:, None] * (CHUNK - 1.0 - idx))[..., None]
    chunk_decay = jnp.exp(log_gamma * CHUNK)[:, None, None]

    def step(state, inp):
        qi, ki, vi = inp
        scores = jnp.einsum('bhnd,bhmd->bhnm', qi, ki) * intra_decay
        o = jnp.einsum('bhnm,bhme->bhne', scores, vi) + \
            jnp.einsum('bhnd,bhde->bhne', qi * q_decay, state)
        state = chunk_decay * state + jnp.einsum('bhmd,bhme->bhde', ki * k_decay, vi)
        return state, o

    s0 = jnp.zeros((b, RET_HEADS, RET_DK, RET_DV), jnp.float32)
    _, oc = lax.scan(step, s0, (qc, kc, vc))
    return oc


def causal_dwconv(u, w, bias):
    y = lax.conv_general_dilated(
        u, w[:, None, :].astype(u.dtype), window_strides=(1,),
        padding=((CONV_W - 1, 0),), dimension_numbers=('NWC', 'WIO', 'NWC'),
        feature_group_count=u.shape[-1])
    return y + bias.astype(u.dtype)


def setup_inputs(seed: int = 0) -> dict:
    key = jax.random.key(seed)
    ks = jax.random.split(key, 20)
    f32 = jnp.float32
    nrm = lambda k, shape, scale: jax.random.normal(k, shape, f32) * scale
    x = jax.random.normal(ks[0], (BATCH, SEQ, D_MODEL), f32)
    offset = jax.random.randint(ks[1], (BATCH, 1), 0, 4096, dtype=jnp.int32)
    positions = offset + jnp.arange(SEQ, dtype=jnp.int32)[None, :]
    return {
        "x": x,
        "positions": positions,
        "ln_pre_mix": 1.0 + nrm(ks[2], (DEPTH, D_MODEL), 0.02),
        "w_in": nrm(ks[3], (DEPTH, D_MODEL, IN_COLS), D_MODEL ** -0.5),
        "w_gla_decay": nrm(ks[4], (DEPTH, GLA_GATE_RANK, GLA_QK), GLA_GATE_RANK ** -0.5),
        "b_gla_decay": nrm(ks[5], (DEPTH, GLA_QK), 0.1),
        "w_gla_out": nrm(ks[6], (DEPTH, GLA_V, D_MODEL), GLA_V ** -0.5),
        "w_ret_out": nrm(ks[7], (DEPTH, RET_V, D_MODEL), RET_V ** -0.5),
        "w_mix_out": nrm(ks[8], (DEPTH, D_MODEL, D_MODEL), D_MODEL ** -0.5),
        "ln_post_mix": 1.0 + nrm(ks[9], (DEPTH, D_MODEL), 0.02),
        "ln_pre_ffn": 1.0 + nrm(ks[10], (DEPTH, D_MODEL), 0.02),
        "w_ffn_up": nrm(ks[11], (DEPTH, D_MODEL, 2 * D_FF), D_MODEL ** -0.5),
        "conv_w": nrm(ks[12], (DEPTH, CONV_W, 2 * D_FF), CONV_W ** -0.5),
        "conv_b": nrm(ks[13], (DEPTH, 2 * D_FF), 0.02),
        "w_ffn_down": nrm(ks[14], (DEPTH, D_FF, D_MODEL), D_FF ** -0.5),
        "ln_post_ffn": 1.0 + nrm(ks[15], (DEPTH, D_MODEL), 0.02),
    }


def reference(x, positions, ln_pre_mix, w_in, w_gla_decay, b_gla_decay, w_gla_out, w_ret_out,
              w_mix_out, ln_post_mix, ln_pre_ffn, w_ffn_up, conv_w, conv_b, w_ffn_down, ln_post_ffn):
    f32 = jnp.float32
    b, s, _ = x.shape
    split_at = [int(v) for v in np.cumsum(IN_SPLITS)[:-1]]
    for layer in range(DEPTH):
        h = rmsnorm(x, ln_pre_mix[layer])
        p = h @ w_in[layer]
        (g_q, g_k, g_v, g_r, g_lr, r_q, r_k, r_v, r_g, a_gla, a_ret) = jnp.split(p, split_at, axis=-1)

        log_a = jax.nn.log_sigmoid((g_lr @ w_gla_decay[layer] + b_gla_decay[layer]).astype(f32)) / GLA_TAU
        o_gla = gla_mixer(g_q.astype(f32), g_k.astype(f32), g_v.astype(f32), log_a)
        o_gla = from_chunks(head_layernorm(o_gla)) * jax.nn.silu(g_r.astype(f32))
        y_gla = o_gla.astype(x.dtype) @ w_gla_out[layer]

        rq = rope(r_q.astype(f32).reshape(b, s, RET_HEADS, RET_DK), positions)
        rk = rope(r_k.astype(f32).reshape(b, s, RET_HEADS, RET_DK), positions)
        o_ret = retention_mixer(rq, rk, r_v.astype(f32))
        o_ret = from_chunks(head_layernorm(o_ret)) * jax.nn.silu(r_g.astype(f32))
        y_ret = o_ret.astype(x.dtype) @ w_ret_out[layer]

        merged = jax.nn.sigmoid(a_gla) * y_gla + jax.nn.sigmoid(a_ret) * y_ret
        x = x + rmsnorm(merged @ w_mix_out[layer], ln_post_mix[layer])

        h = rmsnorm(x, ln_pre_ffn[layer])
        u = causal_dwconv(h @ w_ffn_up[layer], conv_w[layer], conv_b[layer])
        u_val, u_gate = jnp.split(u, 2, axis=-1)
        f = jax.nn.gelu(u_gate, approximate=True) * u_val
        x = x + rmsnorm(f @ w_ffn_down[layer], ln_post_ffn[layer])
    return x
```

```python
import functools
import math

import jax
import jax.numpy as jnp
from jax import lax
from jax.experimental import pallas as pl
from jax.experimental.pallas import tpu as pltpu

F32 = jnp.float32
BF16 = jnp.bfloat16

CHUNK = 64
SUB = 16
EPS = 1e-6
GLA_HEADS = 4
GLA_GATE_RANK = 16
GLA_TAU = 16.0
RET_HEADS = 4
ROPE_BASE = 10000.0
CONV_W = 3
LANES = 128
BF16_ROWS = 16

VMEM_LIMIT = 56 * 1024 * 1024


def _rms(x):
    return x * lax.rsqrt(jnp.mean(x * x, axis=-1, keepdims=True) + EPS)


def _sigmoid(x):
    return 1.0 / (1.0 + jnp.exp(-x))


def _dot(a, b):
    return jnp.dot(a, b, preferred_element_type=F32)


def _dot_nt(a, b):
    return lax.dot_general(a, b, (((1,), (1,)), ((), ())), preferred_element_type=F32)


def _dot_tn(a, b):
    return lax.dot_general(a, b, (((0,), (0,)), ((), ())), preferred_element_type=F32)


def _head_norm_gate(o, gate):
    mu = jnp.mean(o, axis=-1, keepdims=True)
    d = o - mu
    var = jnp.mean(d * d, axis=-1, keepdims=True)
    return d * lax.rsqrt(var + EPS) * (gate * _sigmoid(gate))


def _in_proj_kernel(x_ref, lnw_ref, w_ref, wlr_ref, p_ref, glr_ref, h_ref):
    @pl.when(pl.program_id(1) == 0)
    def _():
        h = (_rms(x_ref[...]) * lnw_ref[...]).astype(BF16)
        h_ref[...] = h
        glr_ref[...] = _dot(h, wlr_ref[...]).astype(BF16)

    p_ref[...] = _dot(h_ref[...], w_ref[...]).astype(BF16)


def _in_proj(x2, lnw, w_main, w_lr, tm, tn):
    t, d = x2.shape
    nc = w_main.shape[1]
    return pl.pallas_call(
        _in_proj_kernel,
        grid=(t // tm, nc // tn),
        in_specs=[
            pl.BlockSpec((tm, d), lambda i, j: (i, 0)),
            pl.BlockSpec((1, d), lambda i, j: (0, 0)),
            pl.BlockSpec((d, tn), lambda i, j: (0, j)),
            pl.BlockSpec((d, LANES), lambda i, j: (0, 0)),
        ],
        out_specs=[
            pl.BlockSpec((tm, tn), lambda i, j: (i, j)),
            pl.BlockSpec((tm, LANES), lambda i, j: (i, 0)),
        ],
        out_shape=[
            jax.ShapeDtypeStruct((t, nc), BF16),
            jax.ShapeDtypeStruct((t, LANES), BF16),
        ],
        scratch_shapes=[pltpu.VMEM((tm, d), BF16)],
        compiler_params=pltpu.CompilerParams(
            dimension_semantics=("parallel", "arbitrary"),
            vmem_limit_bytes=VMEM_LIMIT),
        name="in_proj",
    )(x2, lnw, w_main, w_lr)


def _chunk_cumsum(x):
    rows = x.shape[0]
    r = lax.broadcasted_iota(jnp.int32, (rows, 1), 0) % CHUNK
    s = 1
    while s < CHUNK:
        x = x + jnp.where(r >= s, pltpu.roll(x, s, 0), 0.0)
        s *= 2
    return x


def _pad_rows(a, before, total):
    parts = []
    if before:
        parts.append(jnp.zeros((before, a.shape[1]), a.dtype))
    parts.append(a)
    after = total - before - a.shape[0]
    if after:
        parts.append(jnp.zeros((after, a.shape[1]), a.dtype))
    return jnp.concatenate(parts, axis=0) if len(parts) > 1 else a


def _gla_kernel(q_ref, k_ref, v_ref, r_ref, glr_ref, wdec_ref, bdec_ref, o_ref, state_ref,
                *, dk, dv):
    @pl.when(pl.program_id(1) == 0)
    def _():
        state_ref[...] = jnp.zeros_like(state_ref)

    tb = q_ref.shape[0]
    nsub = CHUNK // SUB
    scale = dk ** -0.5

    z = _dot(glr_ref[...], wdec_ref[...]) + bdec_ref[...]
    log_a = (jnp.minimum(z, 0.0) - jnp.log1p(jnp.exp(-jnp.abs(z)))) * (1.0 / GLA_TAU)
    g_all = _chunk_cumsum(log_a)

    row = lax.broadcasted_iota(jnp.int32, (CHUNK, CHUNK), 0)
    col = lax.broadcasted_iota(jnp.int32, (CHUNK, CHUNK), 1)
    lower = row >= col

    for h in range(GLA_HEADS):
        ks = slice(h * dk, (h + 1) * dk)
        vs = slice(h * dv, (h + 1) * dv)
        st = state_ref[h]
        for c in range(tb // CHUNK):
            rs = slice(c * CHUNK, (c + 1) * CHUNK)
            g = g_all[rs, ks]
            q = q_ref[rs, ks].astype(F32) * scale
            k = k_ref[rs, ks].astype(F32)
            v = v_ref[rs, vs]

            refs = [g[i * SUB:i * SUB + 1, :] for i in range(nsub)]
            e = jnp.exp(g - jnp.concatenate(
                [jnp.broadcast_to(rf, (SUB, dk)) for rf in refs], axis=0))
            qe = (q * e).astype(BF16)
            ke = (k * e).astype(BF16)
            q_lo, k_lo, q_up, k_up = [], [], [], []
            for i in range(nsub):
                n = (i + 1) * SUB
                f = jnp.exp(refs[i] - g[:n, :])
                k_lo.append(_pad_rows((k[:n] * f).astype(BF16), 0, CHUNK))
                q_up.append(_pad_rows((q[:n] * f).astype(BF16), 0, CHUNK))
                q_lo.append(_pad_rows(qe[i * SUB:n], i * SUB, CHUNK))
                k_up.append(_pad_rows(ke[i * SUB:n], i * SUB, CHUNK))
            s_lo = _dot_nt(jnp.concatenate(q_lo, axis=1), jnp.concatenate(k_lo, axis=1))
            s_up = _dot_nt(jnp.concatenate(q_up, axis=1), jnp.concatenate(k_up, axis=1))
            s = jnp.where(lower, s_lo, s_up).astype(BF16)

            o = _dot(s, v) + _dot_nt((q * jnp.exp(g)).astype(BF16), st.astype(BF16))
            g_last = g[CHUNK - 1:CHUNK, :]
            k_dec = (k * jnp.exp(g_last - g)).astype(BF16)
            st = st * jnp.exp(g_last) + _dot_tn(v, k_dec)

            gate = r_ref[rs, vs].astype(F32)
            o_ref[rs, vs] = _head_norm_gate(o, gate).astype(BF16)
        state_ref[h] = st


def _gla(p3, glr3, wdec, bdec, cols, tb):
    b, s, _ = p3.shape
    dk = wdec.shape[1] // GLA_HEADS
    dv = 2 * dk
    qk_w, v_w = GLA_HEADS * dk, GLA_HEADS * dv
    q_blk, k_blk, v_blk, r_blk = (cols["gq"] // qk_w, cols["gk"] // qk_w,
                                  cols["gv"] // v_w, cols["gr"] // v_w)
    return pl.pallas_call(
        functools.partial(_gla_kernel, dk=dk, dv=dv),
        grid=(b, s // tb),
        in_specs=[
            pl.BlockSpec((None, tb, qk_w), lambda i, t: (i, t, q_blk)),
            pl.BlockSpec((None, tb, qk_w), lambda i, t: (i, t, k_blk)),
            pl.BlockSpec((None, tb, v_w), lambda i, t: (i, t, v_blk)),
            pl.BlockSpec((None, tb, v_w), lambda i, t: (i, t, r_blk)),
            pl.BlockSpec((None, tb, LANES), lambda i, t: (i, t, 0)),
            pl.BlockSpec((LANES, qk_w), lambda i, t: (0, 0)),
            pl.BlockSpec((1, qk_w), lambda i, t: (0, 0)),
        ],
        out_specs=pl.BlockSpec((None, tb, v_w), lambda i, t: (i, t, 0)),
        out_shape=jax.ShapeDtypeStruct((b, s, v_w), BF16),
        scratch_shapes=[pltpu.VMEM((GLA_HEADS, dv, dk), F32)],
        compiler_params=pltpu.CompilerParams(
            dimension_semantics=("parallel", "arbitrary"),
            vmem_limit_bytes=VMEM_LIMIT),
        name="gla",
    )(p3, p3, p3, p3, glr3, wdec, bdec)


def _ret_kernel(q_ref, k_ref, v_ref, g_ref, pos_ref, freq_ref, o_ref, state_ref, *, dk, dv):
    @pl.when(pl.program_id(1) == 0)
    def _():
        state_ref[...] = jnp.zeros_like(state_ref)

    tb = q_ref.shape[0]
    half = dk // 2
    ang = pos_ref[...].astype(F32) * freq_ref[...]
    cos, sin = jnp.cos(ang), jnp.sin(ang)

    def rope(x):
        x1, x2 = x[:, :half], x[:, half:]
        return jnp.concatenate([x1 * cos - x2 * sin, x1 * sin + x2 * cos], axis=1)

    row = lax.broadcasted_iota(jnp.int32, (tb, tb), 0)
    col = lax.broadcasted_iota(jnp.int32, (tb, tb), 1)
    visible = (col // CHUNK) <= (row // CHUNK)
    dist = jnp.abs(row - col).astype(F32)
    idx = lax.broadcasted_iota(jnp.int32, (tb, 1), 0).astype(F32)

    for h in range(RET_HEADS):
        log_gamma = math.log(1.0 - 2.0 ** (-5.0 - h))
        decay = jnp.where(visible, jnp.exp(log_gamma * dist), 0.0)
        q_decay = jnp.exp(log_gamma * (idx + 1.0))
        k_decay = jnp.exp(log_gamma * (tb - 1.0 - idx))
        ks = slice(h * dk, (h + 1) * dk)
        vs = slice(h * dv, (h + 1) * dv)
        q = rope(q_ref[:, ks].astype(F32))
        k = rope(k_ref[:, ks].astype(F32)) * (dk ** -0.5)
        v = v_ref[:, vs]
        st = state_ref[h]

        s = (_dot_nt(q.astype(BF16), k.astype(BF16)) * decay).astype(BF16)
        o = _dot(s, v) + _dot((q * q_decay).astype(BF16), st.astype(BF16))
        state_ref[h] = math.exp(log_gamma * tb) * st + _dot_tn((k * k_decay).astype(BF16), v)

        gate = g_ref[:, vs].astype(F32)
        o_ref[:, vs] = _head_norm_gate(o, gate).astype(BF16)


def _retention(p3, pos3, inv_freq, cols, dk, dv, tb):
    b, s, _ = p3.shape
    qk_w, v_w = RET_HEADS * dk, RET_HEADS * dv
    q_blk, k_blk, v_blk, g_blk = (cols["rq"] // qk_w, cols["rk"] // qk_w,
                                  cols["rv"] // v_w, cols["rg"] // v_w)
    return pl.pallas_call(
        functools.partial(_ret_kernel, dk=dk, dv=dv),
        grid=(b, s // tb),
        in_specs=[
            pl.BlockSpec((None, tb, qk_w), lambda i, t: (i, t, q_blk)),
            pl.BlockSpec((None, tb, qk_w), lambda i, t: (i, t, k_blk)),
            pl.BlockSpec((None, tb, v_w), lambda i, t: (i, t, v_blk)),
            pl.BlockSpec((None, tb, v_w), lambda i, t: (i, t, g_blk)),
            pl.BlockSpec((None, tb, 1), lambda i, t: (i, t, 0)),
            pl.BlockSpec((1, dk // 2), lambda i, t: (0, 0)),
        ],
        out_specs=pl.BlockSpec((None, tb, v_w), lambda i, t: (i, t, 0)),
        out_shape=jax.ShapeDtypeStruct((b, s, v_w), BF16),
        scratch_shapes=[pltpu.VMEM((RET_HEADS, dk, dv), F32)],
        compiler_params=pltpu.CompilerParams(
            dimension_semantics=("parallel", "arbitrary"),
            vmem_limit_bytes=VMEM_LIMIT),
        name="retention",
    )(p3, p3, p3, p3, pos3, inv_freq)


def _mix_kernel(og_ref, or_ref, ag_ref, ar_ref, x_ref, wgo_ref, wro_ref, wmo_ref,
                ln_post_ref, ln_pre_ref, x1_ref, h2_ref):
    y_gla = _dot(og_ref[...], wgo_ref[...])
    y_ret = _dot(or_ref[...], wro_ref[...])
    merged = (_sigmoid(ag_ref[...].astype(F32)) * y_gla
              + _sigmoid(ar_ref[...].astype(F32)) * y_ret)
    mo = _dot(merged.astype(BF16), wmo_ref[...])
    x1 = x_ref[...] + _rms(mo) * ln_post_ref[...]
    x1_ref[...] = x1
    h2_ref[...] = (_rms(x1) * ln_pre_ref[...]).astype(BF16)


def _resident(shape):
    return pl.BlockSpec(shape, lambda i: (0,) * len(shape), pipeline_mode=pl.Buffered(1))


def _mix_out(og, orr, p, x2, wgo, wro, wmo, ln_post, ln_pre, cols, tm):
    t, d = x2.shape
    ag_blk, ar_blk = cols["ag"] // d, cols["ar"] // d
    return pl.pallas_call(
        _mix_kernel,
        grid=(t // tm,),
        in_specs=[
            pl.BlockSpec((tm, og.shape[1]), lambda i: (i, 0)),
            pl.BlockSpec((tm, orr.shape[1]), lambda i: (i, 0)),
            pl.BlockSpec((tm, d), lambda i: (i, ag_blk)),
            pl.BlockSpec((tm, d), lambda i: (i, ar_blk)),
            pl.BlockSpec((tm, d), lambda i: (i, 0)),
            _resident(wgo.shape),
            _resident(wro.shape),
            _resident(wmo.shape),
            _resident((1, d)),
            _resident((1, d)),
        ],
        out_specs=[
            pl.BlockSpec((tm, d), lambda i: (i, 0)),
            pl.BlockSpec((tm, d), lambda i: (i, 0)),
        ],
        out_shape=[
            jax.ShapeDtypeStruct((t, d), F32),
            jax.ShapeDtypeStruct((t, d), BF16),
        ],
        compiler_params=pltpu.CompilerParams(
            dimension_semantics=("parallel",),
            vmem_limit_bytes=VMEM_LIMIT),
        name="mix_out",
    )(og, orr, p, p, x2, wgo, wro, wmo, ln_post, ln_pre)


def _gelu_tanh(x):
    return 0.5 * x * (1.0 + jnp.tanh(math.sqrt(2.0 / math.pi) * (x + 0.044715 * (x * x * x))))


def _ffn_kernel(h_ref, halo_ref, x1_ref, wup_ref, cw_ref, cb_ref, wdn_ref, ln_ref, o_ref,
                *, tiles_per_seq, d_ff, tn):
    tm = h_ref.shape[0]
    first = (pl.program_id(0) % tiles_per_seq) == 0
    halo = jnp.where(first, jnp.zeros_like(halo_ref[...]), halo_ref[...])
    h_ext = jnp.concatenate([halo, h_ref[...]], axis=0)

    def conv(u, c0):
        y = u[BF16_ROWS:] * cw_ref[CONV_W - 1:CONV_W, c0:c0 + tn]
        for j in range(CONV_W - 1):
            shift = CONV_W - 1 - j
            y = y + pltpu.roll(u, shift, 0)[BF16_ROWS:] * cw_ref[j:j + 1, c0:c0 + tn]
        return y + cb_ref[:, c0:c0 + tn]

    acc = jnp.zeros((tm, o_ref.shape[1]), F32)
    for c in range(d_ff // tn):
        c0 = c * tn
        u_val = conv(_dot(h_ext, wup_ref[:, c0:c0 + tn]), c0)
        u_gate = conv(_dot(h_ext, wup_ref[:, d_ff + c0:d_ff + c0 + tn]), d_ff + c0)
        f = (_gelu_tanh(u_gate) * u_val).astype(BF16)
        acc = acc + _dot(f, wdn_ref[c0:c0 + tn, :])
    o_ref[...] = x1_ref[...] + _rms(acc) * ln_ref[...]


def _conv_ffn(h2, x1, wup, cw, cb, wdn, ln, seq, tm, tn):
    t, d = x1.shape
    d_ff = wdn.shape[0]
    halo_blocks = tm // BF16_ROWS
    return pl.pallas_call(
        functools.partial(_ffn_kernel, tiles_per_seq=seq // tm, d_ff=d_ff, tn=tn),
        grid=(t // tm,),
        in_specs=[
            pl.BlockSpec((tm, d), lambda i: (i, 0)),
            pl.BlockSpec((BF16_ROWS, d), lambda i: (jnp.maximum(i * halo_blocks - 1, 0), 0)),
            pl.BlockSpec((tm, d), lambda i: (i, 0)),
            _resident(wup.shape),
            _resident(cw.shape),
            _resident(cb.shape),
            _resident(wdn.shape),
            _resident((1, d)),
        ],
        out_specs=pl.BlockSpec((tm, d), lambda i: (i, 0)),
        out_shape=jax.ShapeDtypeStruct((t, d), F32),
        compiler_params=pltpu.CompilerParams(
            dimension_semantics=("parallel",),
            vmem_limit_bytes=VMEM_LIMIT),
        name="conv_ffn",
    )(h2, h2, x1, wup, cw, cb, wdn, ln)


def _largest_tile(n, cap, quantum):
    best = quantum
    for cand in range(quantum, min(n, cap) + 1, quantum):
        if n % cand == 0:
            best = cand
    return best


def kernel(x, positions, ln_pre_mix, w_in, w_gla_decay, b_gla_decay, w_gla_out, w_ret_out,
           w_mix_out, ln_post_mix, ln_pre_ffn, w_ffn_up, conv_w, conv_b, w_ffn_down, ln_post_ffn):
    b, s, d = x.shape
    depth = w_in.shape[0]
    gla_qk = w_gla_decay.shape[2]
    gla_v = w_gla_out.shape[1]
    ret_v = w_ret_out.shape[1]
    ret_qk = ret_v // 2
    d_ff = w_ffn_down.shape[1]
    ret_dk, ret_dv = ret_qk // RET_HEADS, ret_v // RET_HEADS

    names = ("gq", "gk", "gv", "gr", "glr", "rq", "rk", "rv", "rg", "ag", "ar")
    widths = (gla_qk, gla_qk, gla_v, gla_v, GLA_GATE_RANK, ret_qk, ret_qk, ret_v, ret_v, d, d)
    src, off = {}, 0
    for nm, w in zip(names, widths):
        src[nm] = (off, w)
        off += w
    order = ("rv", "rg", "rq", "rk", "gq", "gk", "gv", "gr", "ag", "ar")
    cols, off = {}, 0
    for nm in order:
        cols[nm] = off
        assert off % src[nm][1] == 0
        off += src[nm][1]

    inv_freq = (ROPE_BASE ** (-jnp.arange(0, ret_dk, 2, dtype=F32) / ret_dk))[None, :]
    pos3 = positions.reshape(b, s, 1)

    tm_in = _largest_tile(b * s, 1024, 256)
    tn_in = _largest_tile(off, 1024, 256)
    tb = _largest_tile(s, 256, CHUNK)
    tm_mix = _largest_tile(s, 512, 256)
    tm_ffn = _largest_tile(s, 512, 256)
    tn_ffn = _largest_tile(d_ff, 256, 256)

    x2 = x.reshape(b * s, d)
    for layer in range(depth):
        w = w_in[layer]
        w_main = jnp.concatenate([w[:, src[nm][0]:src[nm][0] + src[nm][1]] for nm in order],
                                 axis=1).astype(BF16)
        lr0 = src["glr"][0]
        w_lr = jnp.pad(w[:, lr0:lr0 + GLA_GATE_RANK],
                       ((0, 0), (0, LANES - GLA_GATE_RANK))).astype(BF16)
        wdec = jnp.pad(w_gla_decay[layer], ((0, LANES - GLA_GATE_RANK), (0, 0))).astype(BF16)
        bdec = b_gla_decay[layer][None, :]

        p, glr = _in_proj(x2, ln_pre_mix[layer][None, :], w_main, w_lr, tm_in, tn_in)
        p3 = p.reshape(b, s, -1)
        og = _gla(p3, glr.reshape(b, s, LANES), wdec, bdec, cols, tb)
        orr = _retention(p3, pos3, inv_freq, cols, ret_dk, ret_dv, tb)
        x1, h2 = _mix_out(og.reshape(b * s, gla_v), orr.reshape(b * s, ret_v), p, x2,
                          w_gla_out[layer].astype(BF16), w_ret_out[layer].astype(BF16),
                          w_mix_out[layer].astype(BF16), ln_post_mix[layer][None, :],
                          ln_pre_ffn[layer][None, :], cols, tm_mix)
        x2 = _conv_ffn(h2, x1, w_ffn_up[layer].astype(BF16), conv_w[layer],
                       conv_b[layer][None, :], w_ffn_down[layer].astype(BF16),
                       ln_post_ffn[layer][None, :], s, tm_ffn, tn_ffn)
    return x2.reshape(b, s, d)
```

```python
import functools
import math

import jax
import jax.numpy as jnp
from jax import lax
from jax.experimental import pallas as pl
from jax.experimental.pallas import tpu as pltpu

F32 = jnp.float32
BF16 = jnp.bfloat16

CHUNK = 64
SUB = 16
EPS = 1e-6
GLA_HEADS = 4
GLA_GATE_RANK = 16
GLA_TAU = 16.0
RET_HEADS = 4
ROPE_BASE = 10000.0
CONV_W = 3
LANES = 128
BF16_ROWS = 16

VMEM_LIMIT = 56 * 1024 * 1024


def _rms(x):
    return x * lax.rsqrt(jnp.mean(x * x, axis=-1, keepdims=True) + EPS)


def _sigmoid(x):
    return 1.0 / (1.0 + jnp.exp(-x))


def _dot(a, b):
    return jnp.dot(a, b, preferred_element_type=F32)


def _dot_nt(a, b):
    return lax.dot_general(a, b, (((1,), (1,)), ((), ())), preferred_element_type=F32)


def _dot_tn(a, b):
    return lax.dot_general(a, b, (((0,), (0,)), ((), ())), preferred_element_type=F32)


def _head_norm_gate(o, gate):
    mu = jnp.mean(o, axis=-1, keepdims=True)
    d = o - mu
    var = jnp.mean(d * d, axis=-1, keepdims=True)
    h = 0.5 * gate
    return d * lax.rsqrt(var + EPS) * (h + h * jnp.tanh(h))


def _in_proj_kernel(x_ref, lnw_ref, w_ref, wlr_ref, p_ref, glr_ref, h_ref):
    @pl.when(pl.program_id(1) == 0)
    def _():
        h = (_rms(x_ref[...]) * lnw_ref[...]).astype(BF16)
        h_ref[...] = h
        glr_ref[...] = _dot(h, wlr_ref[...]).astype(BF16)

    p_ref[...] = _dot(h_ref[...], w_ref[...]).astype(BF16)


def _in_proj(x2, lnw, w_main, w_lr, tm, tn):
    t, d = x2.shape
    nc = w_main.shape[1]
    return pl.pallas_call(
        _in_proj_kernel,
        grid=(t // tm, nc // tn),
        in_specs=[
            pl.BlockSpec((tm, d), lambda i, j: (i, 0)),
            pl.BlockSpec((1, d), lambda i, j: (0, 0)),
            pl.BlockSpec((d, tn), lambda i, j: (0, j)),
            pl.BlockSpec((d, LANES), lambda i, j: (0, 0)),
        ],
        out_specs=[
            pl.BlockSpec((tm, tn), lambda i, j: (i, j)),
            pl.BlockSpec((tm, LANES), lambda i, j: (i, 0)),
        ],
        out_shape=[
            jax.ShapeDtypeStruct((t, nc), BF16),
            jax.ShapeDtypeStruct((t, LANES), BF16),
        ],
        scratch_shapes=[pltpu.VMEM((tm, d), BF16)],
        compiler_params=pltpu.CompilerParams(
            dimension_semantics=("parallel", "arbitrary"),
            vmem_limit_bytes=VMEM_LIMIT),
        name="in_proj",
    )(x2, lnw, w_main, w_lr)


def _chunk_cumsum(x, tri):
    hi = x.astype(BF16)
    r1 = x - hi.astype(F32)
    mid = r1.astype(BF16)
    lo = (r1 - mid.astype(F32)).astype(BF16)
    return _dot(tri, hi) + _dot(tri, mid) + _dot(tri, lo)


def _pad_rows(a, before, total):
    parts = []
    if before:
        parts.append(jnp.zeros((before, a.shape[1]), a.dtype))
    parts.append(a)
    after = total - before - a.shape[0]
    if after:
        parts.append(jnp.zeros((after, a.shape[1]), a.dtype))
    return jnp.concatenate(parts, axis=0) if len(parts) > 1 else a


def _gla_kernel(q_ref, k_ref, v_ref, r_ref, glr_ref, wdec_ref, bdec_ref, o_ref,
                state_ref, tri_ref, *, dk, dv):
    tb = q_ref.shape[0]
    nsub = CHUNK // SUB
    scale = dk ** -0.5

    @pl.when(pl.program_id(1) == 0)
    def _():
        state_ref[...] = jnp.zeros_like(state_ref)
        r = lax.broadcasted_iota(jnp.int32, (tb, tb), 0)
        c = lax.broadcasted_iota(jnp.int32, (tb, tb), 1)
        tri_ref[...] = jnp.where((c <= r) & (c // CHUNK == r // CHUNK), 1.0, 0.0).astype(BF16)

    z = _dot(glr_ref[...], wdec_ref[...]) + bdec_ref[...]
    log_a = (jnp.minimum(z, 0.0) - jnp.log1p(jnp.exp(-jnp.abs(z)))) * (1.0 / GLA_TAU)
    g_all = _chunk_cumsum(log_a, tri_ref[...])

    row = lax.broadcasted_iota(jnp.int32, (CHUNK, CHUNK), 0)
    col = lax.broadcasted_iota(jnp.int32, (CHUNK, CHUNK), 1)
    lower = row >= col

    for h in range(GLA_HEADS):
        ks = slice(h * dk, (h + 1) * dk)
        vs = slice(h * dv, (h + 1) * dv)
        st = state_ref[h]
        for c in range(tb // CHUNK):
            rs = slice(c * CHUNK, (c + 1) * CHUNK)
            g = g_all[rs, ks]
            q = q_ref[rs, ks].astype(F32) * scale
            k = k_ref[rs, ks].astype(F32)
            v = v_ref[rs, vs]

            refs = [g[i * SUB:i * SUB + 1, :] for i in range(nsub)]
            e = jnp.exp(g - jnp.concatenate(
                [jnp.broadcast_to(rf, (SUB, dk)) for rf in refs], axis=0))
            qe = (q * e).astype(BF16)
            ke = (k * e).astype(BF16)
            q_lo, k_lo, q_up, k_up = [], [], [], []
            for i in range(nsub):
                n = (i + 1) * SUB
                f = jnp.exp(refs[i] - g[:n, :])
                kf = k[:n] * f
                k_lo.append(_pad_rows(kf.astype(BF16), 0, CHUNK))
                q_up.append(_pad_rows((q[:n] * f).astype(BF16), 0, CHUNK))
                q_lo.append(_pad_rows(qe[i * SUB:n], i * SUB, CHUNK))
                k_up.append(_pad_rows(ke[i * SUB:n], i * SUB, CHUNK))
            s_lo = _dot_nt(jnp.concatenate(q_lo, axis=1), jnp.concatenate(k_lo, axis=1))
            s_up = _dot_nt(jnp.concatenate(q_up, axis=1), jnp.concatenate(k_up, axis=1))
            s = jnp.where(lower, s_lo, s_up).astype(BF16)

            o = _dot(s, v) + _dot_nt((q * jnp.exp(g)).astype(BF16), st.astype(BF16))
            g_last = g[CHUNK - 1:CHUNK, :]
            k_dec = (kf * jnp.exp(g_last - refs[nsub - 1])).astype(BF16)
            st = st * jnp.exp(g_last) + _dot_tn(v, k_dec)

            gate = r_ref[rs, vs].astype(F32)
            o_ref[rs, vs] = _head_norm_gate(o, gate).astype(BF16)
        state_ref[h] = st


def _gla(p3, glr3, wdec, bdec, cols, tb):
    b, s, _ = p3.shape
    dk = wdec.shape[1] // GLA_HEADS
    dv = 2 * dk
    qk_w, v_w = GLA_HEADS * dk, GLA_HEADS * dv
    q_blk, k_blk, v_blk, r_blk = (cols["gq"] // qk_w, cols["gk"] // qk_w,
                                  cols["gv"] // v_w, cols["gr"] // v_w)
    return pl.pallas_call(
        functools.partial(_gla_kernel, dk=dk, dv=dv),
        grid=(b, s // tb),
        in_specs=[
            pl.BlockSpec((None, tb, qk_w), lambda i, t: (i, t, q_blk)),
            pl.BlockSpec((None, tb, qk_w), lambda i, t: (i, t, k_blk)),
            pl.BlockSpec((None, tb, v_w), lambda i, t: (i, t, v_blk)),
            pl.BlockSpec((None, tb, v_w), lambda i, t: (i, t, r_blk)),
            pl.BlockSpec((None, tb, LANES), lambda i, t: (i, t, 0)),
            pl.BlockSpec((LANES, qk_w), lambda i, t: (0, 0)),
            pl.BlockSpec((1, qk_w), lambda i, t: (0, 0)),
        ],
        out_specs=pl.BlockSpec((None, tb, v_w), lambda i, t: (i, t, 0)),
        out_shape=jax.ShapeDtypeStruct((b, s, v_w), BF16),
        scratch_shapes=[pltpu.VMEM((GLA_HEADS, dv, dk), F32), pltpu.VMEM((tb, tb), BF16)],
        compiler_params=pltpu.CompilerParams(
            dimension_semantics=("parallel", "arbitrary"),
            vmem_limit_bytes=VMEM_LIMIT),
        name="gla",
    )(p3, p3, p3, p3, glr3, wdec, bdec)


def _ret_kernel(q_ref, k_ref, v_ref, g_ref, pos_ref, freq_ref, o_ref,
                state_ref, decay_ref, qk_decay_ref, rot_ref, *, dk, dv):
    tb = q_ref.shape[0]
    half = dk // 2

    @pl.when(pl.program_id(1) == 0)
    def _():
        state_ref[...] = jnp.zeros_like(state_ref)
        row = lax.broadcasted_iota(jnp.int32, (tb, tb), 0)
        col = lax.broadcasted_iota(jnp.int32, (tb, tb), 1)
        visible = (col // CHUNK) <= (row // CHUNK)
        dist = jnp.abs(row - col).astype(F32)
        idx = lax.broadcasted_iota(jnp.int32, (tb, LANES), 0).astype(F32)
        for h in range(RET_HEADS):
            log_gamma = math.log(1.0 - 2.0 ** (-5.0 - h))
            decay_ref[h] = jnp.where(visible, jnp.exp(log_gamma * dist), 0.0) * (dk ** -0.5)
            qk_decay_ref[h, 0] = jnp.exp(log_gamma * (idx + 1.0))
            qk_decay_ref[h, 1] = jnp.exp(log_gamma * (tb - 1.0 - idx)) * (dk ** -0.5)
        ang = lax.broadcasted_iota(jnp.int32, (tb, half), 0).astype(F32) * freq_ref[...]
        rot_ref[0] = jnp.cos(ang)
        rot_ref[1] = jnp.sin(ang)

    ang0 = pos_ref[0:1, :].astype(F32) * freq_ref[...]
    c0, s0 = jnp.cos(ang0), jnp.sin(ang0)
    cr, sr = rot_ref[0], rot_ref[1]
    cos = c0 * cr - s0 * sr
    sin = s0 * cr + c0 * sr

    def rope(x):
        x1, x2 = x[:, :half], x[:, half:]
        return jnp.concatenate([x1 * cos - x2 * sin, x1 * sin + x2 * cos], axis=1)

    def widen(tbl):
        return jnp.concatenate([tbl] * (dk // LANES), axis=1)

    for h in range(RET_HEADS):
        log_gamma = math.log(1.0 - 2.0 ** (-5.0 - h))
        ks = slice(h * dk, (h + 1) * dk)
        vs = slice(h * dv, (h + 1) * dv)
        q = rope(q_ref[:, ks].astype(F32))
        k = rope(k_ref[:, ks].astype(F32))
        v = v_ref[:, vs]
        st = state_ref[h]

        s = (_dot_nt(q.astype(BF16), k.astype(BF16)) * decay_ref[h]).astype(BF16)
        o = _dot(s, v) + _dot((q * widen(qk_decay_ref[h, 0])).astype(BF16), st.astype(BF16))
        state_ref[h] = (math.exp(log_gamma * tb) * st
                        + _dot_tn((k * widen(qk_decay_ref[h, 1])).astype(BF16), v))

        gate = g_ref[:, vs].astype(F32)
        o_ref[:, vs] = _head_norm_gate(o, gate).astype(BF16)


def _retention(p3, pos3, inv_freq, cols, dk, dv, tb):
    b, s, _ = p3.shape
    qk_w, v_w = RET_HEADS * dk, RET_HEADS * dv
    q_blk, k_blk, v_blk, g_blk = (cols["rq"] // qk_w, cols["rk"] // qk_w,
                                  cols["rv"] // v_w, cols["rg"] // v_w)
    return pl.pallas_call(
        functools.partial(_ret_kernel, dk=dk, dv=dv),
        grid=(b, s // tb),
        in_specs=[
            pl.BlockSpec((None, tb, qk_w), lambda i, t: (i, t, q_blk)),
            pl.BlockSpec((None, tb, qk_w), lambda i, t: (i, t, k_blk)),
            pl.BlockSpec((None, tb, v_w), lambda i, t: (i, t, v_blk)),
            pl.BlockSpec((None, tb, v_w), lambda i, t: (i, t, g_blk)),
            pl.BlockSpec((None, 8, 1), lambda i, t: (i, t * (tb // 8), 0)),
            pl.BlockSpec((1, dk // 2), lambda i, t: (0, 0)),
        ],
        out_specs=pl.BlockSpec((None, tb, v_w), lambda i, t: (i, t, 0)),
        out_shape=jax.ShapeDtypeStruct((b, s, v_w), BF16),
        scratch_shapes=[
            pltpu.VMEM((RET_HEADS, dk, dv), F32),
            pltpu.VMEM((RET_HEADS, tb, tb), F32),
            pltpu.VMEM((RET_HEADS, 2, tb, LANES), F32),
            pltpu.VMEM((2, tb, dk // 2), F32),
        ],
        compiler_params=pltpu.CompilerParams(
            dimension_semantics=("parallel", "arbitrary"),
            vmem_limit_bytes=VMEM_LIMIT),
        name="retention",
    )(p3, p3, p3, p3, pos3, inv_freq)


def _mix_kernel(og_ref, or_ref, ag_ref, ar_ref, x_ref, wgo_ref, wro_ref, wmo_ref,
                ln_post_ref, ln_pre_ref, x1_ref, h2_ref):
    y_gla = _dot(og_ref[...], wgo_ref[...])
    y_ret = _dot(or_ref[...], wro_ref[...])
    merged = (_sigmoid(ag_ref[...].astype(F32)) * y_gla
              + _sigmoid(ar_ref[...].astype(F32)) * y_ret)
    mo = _dot(merged.astype(BF16), wmo_ref[...])
    x1 = x_ref[...] + _rms(mo) * ln_post_ref[...]
    x1_ref[...] = x1
    h2_ref[...] = (_rms(x1) * ln_pre_ref[...]).astype(BF16)


def _resident(shape):
    return pl.BlockSpec(shape, lambda i: (0,) * len(shape), pipeline_mode=pl.Buffered(1))


def _mix_out(og, orr, p, x2, wgo, wro, wmo, ln_post, ln_pre, cols, tm):
    t, d = x2.shape
    ag_blk, ar_blk = cols["ag"] // d, cols["ar"] // d
    return pl.pallas_call(
        _mix_kernel,
        grid=(t // tm,),
        in_specs=[
            pl.BlockSpec((tm, og.shape[1]), lambda i: (i, 0)),
            pl.BlockSpec((tm, orr.shape[1]), lambda i: (i, 0)),
            pl.BlockSpec((tm, d), lambda i: (i, ag_blk)),
            pl.BlockSpec((tm, d), lambda i: (i, ar_blk)),
            pl.BlockSpec((tm, d), lambda i: (i, 0)),
            _resident(wgo.shape),
            _resident(wro.shape),
            _resident(wmo.shape),
            _resident((1, d)),
            _resident((1, d)),
        ],
        out_specs=[
            pl.BlockSpec((tm, d), lambda i: (i, 0)),
            pl.BlockSpec((tm, d), lambda i: (i, 0)),
        ],
        out_shape=[
            jax.ShapeDtypeStruct((t, d), F32),
            jax.ShapeDtypeStruct((t, d), BF16),
        ],
        compiler_params=pltpu.CompilerParams(
            dimension_semantics=("parallel",),
            vmem_limit_bytes=VMEM_LIMIT),
        name="mix_out",
    )(og, orr, p, p, x2, wgo, wro, wmo, ln_post, ln_pre)


_GELU_C = math.sqrt(2.0 / math.pi)


def _ffn_kernel(h_ref, halo_ref, x1_ref, wup_ref, cw_ref, cb_ref, wdn_ref, ln_ref, o_ref,
                u0_ref, u1_ref, u2_ref, u3_ref, f_ref, *, tiles_per_seq, d_ff, tn):
    u_ref = (u0_ref, u1_ref, u2_ref, u3_ref)
    tm = h_ref.shape[0]
    first = (pl.program_id(0) % tiles_per_seq) == 0
    halo = jnp.where(first, jnp.zeros_like(halo_ref[...]), halo_ref[...])
    h_ext = jnp.concatenate([halo, h_ref[...]], axis=0)

    def up(c):
        for half in range(2):
            c0 = half * d_ff + c * tn
            u_ref[2 * (c % 2) + half][...] = _dot(h_ext, wup_ref[:, c0:c0 + tn])

    def conv(slot, c0, gain):
        y = cb_ref[:, c0:c0 + tn] * gain
        for j in range(CONV_W):
            r0 = BF16_ROWS - (CONV_W - 1) + j
            y = y + u_ref[slot][r0:r0 + tm, :] * (cw_ref[j:j + 1, c0:c0 + tn] * gain)
        return y

    n_tiles = d_ff // tn
    up(0)
    for c in range(n_tiles):
        if c + 1 < n_tiles:
            up(c + 1)
        half_val = conv(2 * (c % 2), c * tn, 0.5)
        g = conv(2 * (c % 2) + 1, d_ff + c * tn, 1.0)
        t = jnp.tanh(g * (_GELU_C + (_GELU_C * 0.044715) * (g * g)))
        f_ref[:, c * tn:(c + 1) * tn] = ((g + g * t) * half_val).astype(BF16)

    o_ref[...] = x1_ref[...] + _rms(_dot(f_ref[...], wdn_ref[...])) * ln_ref[...]


def _conv_ffn(h2, x1, wup, cw, cb, wdn, ln, seq, tm, tn):
    t, d = x1.shape
    d_ff = wdn.shape[0]
    halo_blocks = tm // BF16_ROWS
    return pl.pallas_call(
        functools.partial(_ffn_kernel, tiles_per_seq=seq // tm, d_ff=d_ff, tn=tn),
        grid=(t // tm,),
        in_specs=[
            pl.BlockSpec((tm, d), lambda i: (i, 0)),
            pl.BlockSpec((BF16_ROWS, d), lambda i: (jnp.maximum(i * halo_blocks - 1, 0), 0)),
            pl.BlockSpec((tm, d), lambda i: (i, 0)),
            _resident(wup.shape),
            _resident(cw.shape),
            _resident(cb.shape),
            _resident(wdn.shape),
            _resident((1, d)),
        ],
        out_specs=pl.BlockSpec((tm, d), lambda i: (i, 0)),
        out_shape=jax.ShapeDtypeStruct((t, d), F32),
        scratch_shapes=[pltpu.VMEM((BF16_ROWS + tm, tn), F32)] * 4
        + [pltpu.VMEM((tm, d_ff), BF16)],
        compiler_params=pltpu.CompilerParams(
            dimension_semantics=("parallel",),
            vmem_limit_bytes=VMEM_LIMIT),
        name="conv_ffn",
    )(h2, h2, x1, wup, cw, cb, wdn, ln)


def _largest_tile(n, cap, quantum):
    best = quantum
    for cand in range(quantum, min(n, cap) + 1, quantum):
        if n % cand == 0:
            best = cand
    return best


def kernel(x, positions, ln_pre_mix, w_in, w_gla_decay, b_gla_decay, w_gla_out, w_ret_out,
           w_mix_out, ln_post_mix, ln_pre_ffn, w_ffn_up, conv_w, conv_b, w_ffn_down, ln_post_ffn):
    b, s, d = x.shape
    depth = w_in.shape[0]
    gla_qk = w_gla_decay.shape[2]
    gla_v = w_gla_out.shape[1]
    ret_v = w_ret_out.shape[1]
    ret_qk = ret_v // 2
    d_ff = w_ffn_down.shape[1]
    ret_dk, ret_dv = ret_qk // RET_HEADS, ret_v // RET_HEADS

    names = ("gq", "gk", "gv", "gr", "glr", "rq", "rk", "rv", "rg", "ag", "ar")
    widths = (gla_qk, gla_qk, gla_v, gla_v, GLA_GATE_RANK, ret_qk, ret_qk, ret_v, ret_v, d, d)
    src, off = {}, 0
    for nm, w in zip(names, widths):
        src[nm] = (off, w)
        off += w
    order = ("rv", "rg", "rq", "rk", "gq", "gk", "gv", "gr", "ag", "ar")
    cols, off = {}, 0
    for nm in order:
        cols[nm] = off
        assert off % src[nm][1] == 0
        off += src[nm][1]

    inv_freq = (ROPE_BASE ** (-jnp.arange(0, ret_dk, 2, dtype=F32) / ret_dk))[None, :]
    pos3 = positions.reshape(b, s, 1)

    tm_in = _largest_tile(b * s, 2048, 256)
    tn_in = _largest_tile(off, 1408, 128)
    tb = _largest_tile(s, 256, CHUNK)
    tm_mix = _largest_tile(s, 512, 256)
    tm_ffn = _largest_tile(s, 512, 256)
    tn_ffn = _largest_tile(d_ff, 256, 256)

    x2 = x.reshape(b * s, d)
    for layer in range(depth):
        w = w_in[layer]
        w_main = jnp.concatenate([w[:, src[nm][0]:src[nm][0] + src[nm][1]] for nm in order],
                                 axis=1).astype(BF16)
        lr0 = src["glr"][0]
        w_lr = jnp.pad(w[:, lr0:lr0 + GLA_GATE_RANK],
                       ((0, 0), (0, LANES - GLA_GATE_RANK))).astype(BF16)
        wdec = jnp.pad(w_gla_decay[layer], ((0, LANES - GLA_GATE_RANK), (0, 0))).astype(BF16)
        bdec = b_gla_decay[layer][None, :]

        p, glr = _in_proj(x2, ln_pre_mix[layer][None, :], w_main, w_lr, tm_in, tn_in)
        p3 = p.reshape(b, s, -1)
        og = _gla(p3, glr.reshape(b, s, LANES), wdec, bdec, cols, tb)
        orr = _retention(p3, pos3, inv_freq, cols, ret_dk, ret_dv, tb)
        x1, h2 = _mix_out(og.reshape(b * s, gla_v), orr.reshape(b * s, ret_v), p, x2,
                          w_gla_out[layer].astype(BF16), w_ret_out[layer].astype(BF16),
                          w_mix_out[layer].astype(BF16), ln_post_mix[layer][None, :],
                          ln_pre_ffn[layer][None, :], cols, tm_mix)
        x2 = _conv_ffn(h2, x1, w_ffn_up[layer].astype(BF16), conv_w[layer],
                       conv_b[layer][None, :], w_ffn_down[layer].astype(BF16),
                       ln_post_ffn[layer][None, :], s, tm_ffn, tn_ffn)
    return x2.reshape(b, s, d)
```

```python
import functools
import math

import jax
import jax.numpy as jnp
from jax import lax
from jax.experimental import pallas as pl
from jax.experimental.pallas import tpu as pltpu

F32 = jnp.float32
BF16 = jnp.bfloat16

CHUNK = 64
SUB = 16
EPS = 1e-6
GLA_HEADS = 4
GLA_GATE_RANK = 16
GLA_TAU = 16.0
RET_HEADS = 4
ROPE_BASE = 10000.0
CONV_W = 3
LANES = 128
BF16_ROWS = 16
MIX_ROWS = 256

VMEM_LIMIT = 56 * 1024 * 1024


def _rms(x):
    return x * lax.rsqrt(jnp.mean(x * x, axis=-1, keepdims=True) + EPS)


def _sigmoid(x):
    return 0.5 + 0.5 * jnp.tanh(0.5 * x)


def _dot(a, b):
    return jnp.dot(a, b, preferred_element_type=F32)


def _dot_nt(a, b):
    return lax.dot_general(a, b, (((1,), (1,)), ((), ())), preferred_element_type=F32)


def _dot_tn(a, b):
    return lax.dot_general(a, b, (((0,), (0,)), ((), ())), preferred_element_type=F32)


def _head_norm_gate(o, gate):
    mu = jnp.mean(o, axis=-1, keepdims=True)
    d = o - mu
    var = jnp.mean(d * d, axis=-1, keepdims=True)
    h = 0.5 * gate
    return (d * lax.rsqrt(var + EPS)).astype(BF16) * (h + h * jnp.tanh(h))


def _in_proj_kernel(x_ref, lnw_ref, w_ref, wlr_ref, p_ref, glr_ref, h_ref):
    @pl.when(pl.program_id(1) == 0)
    def _():
        h = (_rms(x_ref[...]) * lnw_ref[...]).astype(BF16)
        h_ref[...] = h
        glr_ref[...] = _dot(h, wlr_ref[...]).astype(BF16)

    p_ref[...] = _dot(h_ref[...], w_ref[...]).astype(BF16)


def _in_proj(x2, lnw, w_main, w_lr, tm, tn):
    t, d = x2.shape
    nc = w_main.shape[1]
    return pl.pallas_call(
        _in_proj_kernel,
        grid=(t // tm, nc // tn),
        in_specs=[
            pl.BlockSpec((tm, d), lambda i, j: (i, 0)),
            pl.BlockSpec((1, d), lambda i, j: (0, 0)),
            pl.BlockSpec((d, tn), lambda i, j: (0, j)),
            pl.BlockSpec((d, LANES), lambda i, j: (0, 0)),
        ],
        out_specs=[
            pl.BlockSpec((tm, tn), lambda i, j: (i, j)),
            pl.BlockSpec((tm, LANES), lambda i, j: (i, 0)),
        ],
        out_shape=[
            jax.ShapeDtypeStruct((t, nc), BF16),
            jax.ShapeDtypeStruct((t, LANES), BF16),
        ],
        scratch_shapes=[pltpu.VMEM((tm, d), BF16)],
        compiler_params=pltpu.CompilerParams(
            dimension_semantics=("parallel", "arbitrary"),
            vmem_limit_bytes=VMEM_LIMIT),
        name="in_proj",
    )(x2, lnw, w_main, w_lr)


def _chunk_cumsum(x, tri):
    hi = x.astype(BF16)
    r1 = x - hi.astype(F32)
    mid = r1.astype(BF16)
    lo = (r1 - mid.astype(F32)).astype(BF16)
    return _dot(tri, hi) + _dot(tri, mid) + _dot(tri, lo)


def _pad_rows(a, before, total):
    parts = []
    if before:
        parts.append(jnp.zeros((before, a.shape[1]), a.dtype))
    parts.append(a)
    after = total - before - a.shape[0]
    if after:
        parts.append(jnp.zeros((after, a.shape[1]), a.dtype))
    return jnp.concatenate(parts, axis=0) if len(parts) > 1 else a


def _gla_kernel(q_ref, k_ref, v_ref, glr0_ref, glr_next_ref, wdec_ref, bdec_ref, o_ref,
                state_ref, tri_ref, g_ref, *, dk, dv):
    tb = q_ref.shape[0]
    nsub = CHUNK // SUB
    scale = dk ** -0.5
    t = pl.program_id(1)

    def log_decay_prefix(glr):
        z = _dot(glr, wdec_ref[...]) + bdec_ref[...]
        log_a = (jnp.minimum(z, 0.0) - jnp.log1p(jnp.exp(-jnp.abs(z)))) * (1.0 / GLA_TAU)
        return _chunk_cumsum(log_a, tri_ref[...])

    @pl.when(t == 0)
    def _():
        state_ref[...] = jnp.zeros_like(state_ref)
        r = lax.broadcasted_iota(jnp.int32, (tb, tb), 0)
        c = lax.broadcasted_iota(jnp.int32, (tb, tb), 1)
        tri_ref[...] = jnp.where((c <= r) & (c // CHUNK == r // CHUNK), 1.0, 0.0).astype(BF16)
        g_ref[0] = log_decay_prefix(glr0_ref[...])

    row = lax.broadcasted_iota(jnp.int32, (CHUNK, CHUNK), 0)
    col = lax.broadcasted_iota(jnp.int32, (CHUNK, CHUNK), 1)
    lower = row >= col
    n_chunks = tb // CHUNK
    units = [(h, c) for c in range(n_chunks) for h in range(GLA_HEADS)]

    def rows(c):
        return slice(c * CHUNK, (c + 1) * CHUNK)

    def lanes(h, w):
        return slice(h * w, (h + 1) * w)


    prep = {}
    for h, c in units:
        g = g_ref[t % 2, rows(c), lanes(h, dk)]
        q = q_ref[rows(c), lanes(h, dk)].astype(F32) * scale
        k = k_ref[rows(c), lanes(h, dk)].astype(F32)
        refs = [g[i * SUB:i * SUB + 1, :] for i in range(nsub)]
        e = jnp.exp(g - jnp.concatenate(
            [jnp.broadcast_to(rf, (SUB, dk)) for rf in refs], axis=0))
        qe = (q * e).astype(BF16)
        ke = (k * e).astype(BF16)
        q_lo, k_lo, q_up, k_up = [], [], [], []
        for i in range(nsub):
            n = (i + 1) * SUB
            f = jnp.exp(refs[i] - g[:n, :])
            kf = k[:n] * f
            k_lo.append(_pad_rows(kf.astype(BF16), 0, CHUNK))
            q_up.append(_pad_rows((q[:n] * f).astype(BF16), 0, CHUNK))
            q_lo.append(_pad_rows(qe[i * SUB:n], i * SUB, CHUNK))
            k_up.append(_pad_rows(ke[i * SUB:n], i * SUB, CHUNK))
        g_last = g[CHUNK - 1:CHUNK, :]
        prep[h, c] = dict(
            q_lo=jnp.concatenate(q_lo, axis=1), k_lo=jnp.concatenate(k_lo, axis=1),
            q_up=jnp.concatenate(q_up, axis=1), k_up=jnp.concatenate(k_up, axis=1),
            q_in=(q * jnp.exp(g)).astype(BF16),
            k_dec=(kf * jnp.exp(g_last - refs[nsub - 1])).astype(BF16),
            decay=jnp.exp(g_last))

    scores, incr = {}, {}
    for h, c in units:
        p = prep[h, c]
        s_lo = _dot_nt(p["q_lo"], p["k_lo"])
        s_up = _dot_nt(p["q_up"], p["k_up"])
        scores[h, c] = jnp.where(lower, s_lo, s_up).astype(BF16)
        incr[h, c] = _dot_tn(p["k_dec"], v_ref[rows(c), lanes(h, dv)])

    g_next = log_decay_prefix(glr_next_ref[...])

    state_in = {}
    for h in range(GLA_HEADS):
        st = state_ref[h]
        for c in range(n_chunks):
            state_in[h, c] = st.astype(BF16)
            col = jnp.transpose(jnp.broadcast_to(prep[h, c]["decay"], (dk, dk)))
            st = st * jnp.concatenate([col] * (dv // dk), axis=1) + incr[h, c]
        state_ref[h] = st

    for h, c in units:
        o = _dot(jnp.concatenate([prep[h, c]["q_in"], scores[h, c]], axis=1),
                 jnp.concatenate([state_in[h, c], v_ref[rows(c), lanes(h, dv)]], axis=0))
        o_ref[rows(c), lanes(h, dv)] = o.astype(BF16)

    g_ref[(t + 1) % 2] = g_next


def _gla(p3, glr3, wdec, bdec, cols, tb):
    b, s, _ = p3.shape
    dk = wdec.shape[1] // GLA_HEADS
    dv = 2 * dk
    qk_w, v_w = GLA_HEADS * dk, GLA_HEADS * dv
    q_blk, k_blk, v_blk = cols["gq"] // qk_w, cols["gk"] // qk_w, cols["gv"] // v_w
    last = s // tb - 1
    return pl.pallas_call(
        functools.partial(_gla_kernel, dk=dk, dv=dv),
        grid=(b, s // tb),
        in_specs=[
            pl.BlockSpec((None, tb, qk_w), lambda i, t: (i, t, q_blk)),
            pl.BlockSpec((None, tb, qk_w), lambda i, t: (i, t, k_blk)),
            pl.BlockSpec((None, tb, v_w), lambda i, t: (i, t, v_blk)),
            pl.BlockSpec((None, tb, LANES), lambda i, t: (i, 0, 0)),
            pl.BlockSpec((None, tb, LANES), lambda i, t: (i, jnp.minimum(t + 1, last), 0)),
            pl.BlockSpec((LANES, qk_w), lambda i, t: (0, 0)),
            pl.BlockSpec((1, qk_w), lambda i, t: (0, 0)),
        ],
        out_specs=pl.BlockSpec((None, tb, v_w), lambda i, t: (i, t, 0)),
        out_shape=jax.ShapeDtypeStruct((b, s, v_w), BF16),
        scratch_shapes=[
            pltpu.VMEM((GLA_HEADS, dk, dv), F32),
            pltpu.VMEM((tb, tb), BF16),
            pltpu.VMEM((2, tb, qk_w), F32),
        ],
        compiler_params=pltpu.CompilerParams(
            dimension_semantics=("parallel", "arbitrary"),
            vmem_limit_bytes=VMEM_LIMIT),
        name="gla",
    )(p3, p3, p3, glr3, glr3, wdec, bdec)


def _ret_kernel(pos0_ref, q_ref, k_ref, *rest, dk, dv, n_parts):
    v_refs = rest[:n_parts]
    freq_ref, o_ref, state_ref, decay_ref, qk_decay_ref, rot_ref = rest[n_parts:]
    tb = q_ref.shape[0]
    half = dk // 2
    heads_per_part = RET_HEADS // n_parts

    @pl.when(pl.program_id(1) == 0)
    def _():
        state_ref[...] = jnp.zeros_like(state_ref)
        row = lax.broadcasted_iota(jnp.int32, (tb, tb), 0)
        col = lax.broadcasted_iota(jnp.int32, (tb, tb), 1)
        visible = (col // CHUNK) <= (row // CHUNK)
        dist = jnp.abs(row - col).astype(F32)
        idx = lax.broadcasted_iota(jnp.int32, (tb, LANES), 0).astype(F32)
        for h in range(RET_HEADS):
            log_gamma = math.log(1.0 - 2.0 ** (-5.0 - h))
            decay_ref[h] = jnp.where(visible, jnp.exp(log_gamma * dist), 0.0) * (dk ** -0.5)
            qk_decay_ref[h, 0] = jnp.exp(log_gamma * (idx + 1.0))
            qk_decay_ref[h, 1] = jnp.exp(log_gamma * (tb - 1.0 - idx)) * (dk ** -0.5)
        ang = lax.broadcasted_iota(jnp.int32, (tb, half), 0).astype(F32) * freq_ref[...]
        rot_ref[0] = jnp.cos(ang)
        rot_ref[1] = jnp.sin(ang)

    pos0 = pos0_ref[pl.program_id(0), pl.program_id(1)]
    ang0 = pos0.astype(F32) * freq_ref[...]
    c0, s0 = jnp.cos(ang0), jnp.sin(ang0)
    cr, sr = rot_ref[0], rot_ref[1]
    cos = c0 * cr - s0 * sr
    sin = s0 * cr + c0 * sr

    def rope(x):
        x1, x2 = x[:, :half], x[:, half:]
        return jnp.concatenate([x1 * cos - x2 * sin, x1 * sin + x2 * cos], axis=1)

    def widen(tbl):
        return jnp.concatenate([tbl] * (dk // LANES), axis=1)

    for h in range(RET_HEADS):
        log_gamma = math.log(1.0 - 2.0 ** (-5.0 - h))
        ks = slice(h * dk, (h + 1) * dk)
        part, vs = h // heads_per_part, slice((h % heads_per_part) * dv,
                                              (h % heads_per_part + 1) * dv)
        q = rope(q_ref[:, ks].astype(F32))
        k = rope(k_ref[:, ks].astype(F32))
        v = v_refs[part][:, vs]
        st = state_ref[h]

        s = (_dot_nt(q.astype(BF16), k.astype(BF16)) * decay_ref[h]).astype(BF16)
        o = _dot(s, v) + _dot((q * widen(qk_decay_ref[h, 0])).astype(BF16), st.astype(BF16))
        state_ref[h] = (math.exp(log_gamma * tb) * st
                        + _dot_tn((k * widen(qk_decay_ref[h, 1])).astype(BF16), v))

        o_ref[:, h * dv:(h + 1) * dv] = o.astype(BF16)


def _retention(p3, pos0, inv_freq, cols, dk, dv, tb):
    b, s, _ = p3.shape
    qk_w, v_w = RET_HEADS * dk, RET_HEADS * dv
    q_blk, k_blk = cols["rq"] // qk_w, cols["rk"] // qk_w
    part_w = math.gcd(cols["rv"], v_w)
    n_parts = v_w // part_w
    assert part_w % dv == 0

    def part_spec(off, j):
        blk = off // part_w + j
        return pl.BlockSpec((None, tb, part_w), lambda i, t, pos: (i, t, blk))

    grid_spec = pltpu.PrefetchScalarGridSpec(
        num_scalar_prefetch=1,
        grid=(b, s // tb),
        in_specs=[
            pl.BlockSpec((None, tb, qk_w), lambda i, t, pos: (i, t, q_blk)),
            pl.BlockSpec((None, tb, qk_w), lambda i, t, pos: (i, t, k_blk)),
            *[part_spec(cols["rv"], j) for j in range(n_parts)],
            pl.BlockSpec((1, dk // 2), lambda i, t, pos: (0, 0)),
        ],
        out_specs=pl.BlockSpec((None, tb, v_w), lambda i, t, pos: (i, t, 0)),
        scratch_shapes=[
            pltpu.VMEM((RET_HEADS, dk, dv), F32),
            pltpu.VMEM((RET_HEADS, tb, tb), F32),
            pltpu.VMEM((RET_HEADS, 2, tb, LANES), F32),
            pltpu.VMEM((2, tb, dk // 2), F32),
        ],
    )
    return pl.pallas_call(
        functools.partial(_ret_kernel, dk=dk, dv=dv, n_parts=n_parts),
        grid_spec=grid_spec,
        out_shape=jax.ShapeDtypeStruct((b, s, v_w), BF16),
        compiler_params=pltpu.CompilerParams(
            dimension_semantics=("parallel", "arbitrary"),
            vmem_limit_bytes=VMEM_LIMIT),
        name="retention",
    )(pos0, p3, p3, *([p3] * n_parts), inv_freq)


def _mix_kernel(*refs, gla_dv, ret_dv, n_rg):
    og_ref, or_ref, gr_ref = refs[:3]
    rg_refs = refs[3:3 + n_rg]
    (ag_ref, ar_ref, x_ref, wgo_ref, wro_ref, wmo_ref,
     ln_post_ref, ln_pre_ref, x1_ref, h2_ref) = refs[3 + n_rg:]

    def normed(o_ref, gate_refs, dv, rs):
        heads_per_ref = gate_refs[0].shape[1] // dv
        outs = []
        for h in range(o_ref.shape[1] // dv):
            gate = gate_refs[h // heads_per_ref][rs, (h % heads_per_ref) * dv:
                                                 (h % heads_per_ref + 1) * dv]
            outs.append(_head_norm_gate(o_ref[rs, h * dv:(h + 1) * dv].astype(F32), gate))
        return jnp.concatenate(outs, axis=1)

    tm = x_ref.shape[0]
    for r in range(tm // MIX_ROWS):
        rs = slice(r * MIX_ROWS, (r + 1) * MIX_ROWS)
        y_gla = _dot(normed(og_ref, (gr_ref,), gla_dv, rs), wgo_ref[...])
        y_ret = _dot(normed(or_ref, rg_refs, ret_dv, rs), wro_ref[...])
        merged = (_sigmoid(ag_ref[rs, :]) * y_gla.astype(BF16)
                  + _sigmoid(ar_ref[rs, :]) * y_ret.astype(BF16))
        mo = _dot(merged, wmo_ref[...])
        x1 = x_ref[rs, :] + _rms(mo) * ln_post_ref[...]
        x1_ref[rs, :] = x1
        h2_ref[rs, :] = (_rms(x1) * ln_pre_ref[...]).astype(BF16)


def _resident(shape):
    return pl.BlockSpec(shape, lambda i: (0,) * len(shape), pipeline_mode=pl.Buffered(1))


def _mix_out(og, orr, p, x2, wgo, wro, wmo, ln_post, ln_pre, cols, tm):
    t, d = x2.shape
    gla_v, ret_v = og.shape[1], orr.shape[1]
    rg_w = math.gcd(cols["rg"], ret_v)
    n_rg = ret_v // rg_w

    def col_spec(off, w):
        blk = off // w
        return pl.BlockSpec((tm, w), lambda i: (i, blk))

    return pl.pallas_call(
        functools.partial(_mix_kernel, gla_dv=gla_v // GLA_HEADS, ret_dv=ret_v // RET_HEADS,
                          n_rg=n_rg),
        grid=(t // tm,),
        in_specs=[
            pl.BlockSpec((tm, gla_v), lambda i: (i, 0)),
            pl.BlockSpec((tm, ret_v), lambda i: (i, 0)),
            col_spec(cols["gr"], gla_v),
            *[col_spec(cols["rg"] + j * rg_w, rg_w) for j in range(n_rg)],
            col_spec(cols["ag"], d),
            col_spec(cols["ar"], d),
            pl.BlockSpec((tm, d), lambda i: (i, 0)),
            _resident(wgo.shape),
            _resident(wro.shape),
            _resident(wmo.shape),
            _resident((1, d)),
            _resident((1, d)),
        ],
        out_specs=[
            pl.BlockSpec((tm, d), lambda i: (i, 0)),
            pl.BlockSpec((tm, d), lambda i: (i, 0)),
        ],
        out_shape=[
            jax.ShapeDtypeStruct((t, d), F32),
            jax.ShapeDtypeStruct((t, d), BF16),
        ],
        compiler_params=pltpu.CompilerParams(
            dimension_semantics=("parallel",),
            vmem_limit_bytes=VMEM_LIMIT),
        name="mix_out",
    )(og, orr, p, *([p] * n_rg), p, p, x2, wgo, wro, wmo, ln_post, ln_pre)


_GELU_C = math.sqrt(2.0 / math.pi)


def _ffn_kernel(h_ref, halo_ref, x1_ref, wup_ref, cw_ref, cb_ref, wdn_ref, ln_ref, o_ref,
                u0_ref, u1_ref, u2_ref, u3_ref, f_ref, *, tiles_per_seq, d_ff, tn):
    u_ref = (u0_ref, u1_ref, u2_ref, u3_ref)
    tm = h_ref.shape[0]
    first = (pl.program_id(0) % tiles_per_seq) == 0
    halo = jnp.where(first, jnp.zeros_like(halo_ref[...]), halo_ref[...])
    h_ext = jnp.concatenate([halo, h_ref[...]], axis=0)

    def up(c):
        for half in range(2):
            c0 = half * d_ff + c * tn
            u_ref[2 * (c % 2) + half][...] = _dot(h_ext, wup_ref[:, c0:c0 + tn])

    def conv(slot, c0, gain):
        y = cb_ref[:, c0:c0 + tn] * gain
        for j in range(CONV_W):
            r0 = BF16_ROWS - (CONV_W - 1) + j
            y = y + u_ref[slot][r0:r0 + tm, :] * (cw_ref[j:j + 1, c0:c0 + tn] * gain)
        return y

    n_tiles = d_ff // tn
    up(0)
    for c in range(n_tiles):
        if c + 1 < n_tiles:
            up(c + 1)
        half_val = conv(2 * (c % 2), c * tn, 0.5)
        g = conv(2 * (c % 2) + 1, d_ff + c * tn, 1.0)
        t = jnp.tanh(g * (_GELU_C + (_GELU_C * 0.044715) * (g * g)))
        f_ref[:, c * tn:(c + 1) * tn] = ((g + g * t) * half_val).astype(BF16)

    o_ref[...] = x1_ref[...] + _rms(_dot(f_ref[...], wdn_ref[...])) * ln_ref[...]


def _conv_ffn(h2, x1, wup, cw, cb, wdn, ln, seq, tm, tn):
    t, d = x1.shape
    d_ff = wdn.shape[0]
    halo_blocks = tm // BF16_ROWS
    return pl.pallas_call(
        functools.partial(_ffn_kernel, tiles_per_seq=seq // tm, d_ff=d_ff, tn=tn),
        grid=(t // tm,),
        in_specs=[
            pl.BlockSpec((tm, d), lambda i: (i, 0)),
            pl.BlockSpec((BF16_ROWS, d), lambda i: (jnp.maximum(i * halo_blocks - 1, 0), 0)),
            pl.BlockSpec((tm, d), lambda i: (i, 0)),
            _resident(wup.shape),
            _resident(cw.shape),
            _resident(cb.shape),
            _resident(wdn.shape),
            _resident((1, d)),
        ],
        out_specs=pl.BlockSpec((tm, d), lambda i: (i, 0)),
        out_shape=jax.ShapeDtypeStruct((t, d), F32),
        scratch_shapes=[pltpu.VMEM((BF16_ROWS + tm, tn), F32)] * 4
        + [pltpu.VMEM((tm, d_ff), BF16)],
        compiler_params=pltpu.CompilerParams(
            dimension_semantics=("parallel",),
            vmem_limit_bytes=VMEM_LIMIT),
        name="conv_ffn",
    )(h2, h2, x1, wup, cw, cb, wdn, ln)


def _largest_tile(n, cap, quantum):
    best = quantum
    for cand in range(quantum, min(n, cap) + 1, quantum):
        if n % cand == 0:
            best = cand
    return best


def kernel(x, positions, ln_pre_mix, w_in, w_gla_decay, b_gla_decay, w_gla_out, w_ret_out,
           w_mix_out, ln_post_mix, ln_pre_ffn, w_ffn_up, conv_w, conv_b, w_ffn_down, ln_post_ffn):
    b, s, d = x.shape
    depth = w_in.shape[0]
    gla_qk = w_gla_decay.shape[2]
    gla_v = w_gla_out.shape[1]
    ret_v = w_ret_out.shape[1]
    ret_qk = ret_v // 2
    d_ff = w_ffn_down.shape[1]
    ret_dk, ret_dv = ret_qk // RET_HEADS, ret_v // RET_HEADS

    names = ("gq", "gk", "gv", "gr", "glr", "rq", "rk", "rv", "rg", "ag", "ar")
    widths = (gla_qk, gla_qk, gla_v, gla_v, GLA_GATE_RANK, ret_qk, ret_qk, ret_v, ret_v, d, d)
    lr0 = sum(widths[:names.index("glr")])
    cols, off = {}, 0
    for nm, w in zip(names, widths):
        if nm != "glr":
            cols[nm] = off
            off += w
    n_main = off

    inv_freq = (ROPE_BASE ** (-jnp.arange(0, ret_dk, 2, dtype=F32) / ret_dk))[None, :]

    tm_in = _largest_tile(b * s, 1024, 256)
    tn_in = _largest_tile(n_main, 2816, 256)
    tb = _largest_tile(s, 256, CHUNK)
    tm_mix = _largest_tile(s, 512, 256)
    tm_ffn = _largest_tile(s, 512, 256)
    tn_ffn = _largest_tile(d_ff, 256, 256)
    pos0 = positions[:, ::tb]

    x2 = x.reshape(b * s, d)
    for layer in range(depth):
        w = w_in[layer]
        w_main = jnp.concatenate([w[:, :lr0], w[:, lr0 + GLA_GATE_RANK:]], axis=1).astype(BF16)
        w_lr = jnp.pad(w[:, lr0:lr0 + GLA_GATE_RANK],
                       ((0, 0), (0, LANES - GLA_GATE_RANK))).astype(BF16)
        wdec = jnp.pad(w_gla_decay[layer], ((0, LANES - GLA_GATE_RANK), (0, 0))).astype(BF16)
        bdec = b_gla_decay[layer][None, :]

        p, glr = _in_proj(x2, ln_pre_mix[layer][None, :], w_main, w_lr, tm_in, tn_in)
        p3 = p.reshape(b, s, -1)
        og = _gla(p3, glr.reshape(b, s, LANES), wdec, bdec, cols, tb)
        orr = _retention(p3, pos0, inv_freq, cols, ret_dk, ret_dv, tb)
        x1, h2 = _mix_out(og.reshape(b * s, gla_v), orr.reshape(b * s, ret_v), p, x2,
                          w_gla_out[layer].astype(BF16), w_ret_out[layer].astype(BF16),
                          w_mix_out[layer].astype(BF16), ln_post_mix[layer][None, :],
                          ln_pre_ffn[layer][None, :], cols, tm_mix)
        x2 = _conv_ffn(h2, x1, w_ffn_up[layer].astype(BF16), conv_w[layer],
                       conv_b[layer][None, :], w_ffn_down[layer].astype(BF16),
                       ln_post_ffn[layer][None, :], s, tm_ffn, tn_ffn)
    return x2.reshape(b, s, d)
```

```python
import functools
import math

import jax
import jax.numpy as jnp
from jax import lax
from jax.experimental import pallas as pl
from jax.experimental.pallas import tpu as pltpu

F32 = jnp.float32
BF16 = jnp.bfloat16

CHUNK = 64
SUB = 16
EPS = 1e-6
GLA_HEADS = 4
GLA_GATE_RANK = 16
GLA_TAU = 16.0
RET_HEADS = 4
ROPE_BASE = 10000.0
CONV_W = 3
LANES = 128
BF16_ROWS = 16
MIX_ROWS = 256
CONV_PHASES = 4

VMEM_LIMIT = 56 * 1024 * 1024


def _rms(x):
    return x * lax.rsqrt(jnp.mean(x * x, axis=-1, keepdims=True) + EPS)


def _sigmoid(x):
    return 0.5 + 0.5 * jnp.tanh(0.5 * x)


def _dot(a, b):
    return jnp.dot(a, b, preferred_element_type=F32)


def _dot_nt(a, b):
    return lax.dot_general(a, b, (((1,), (1,)), ((), ())), preferred_element_type=F32)


def _dot_tn(a, b):
    return lax.dot_general(a, b, (((0,), (0,)), ((), ())), preferred_element_type=F32)


def _head_norm_gate(o, gate):
    mu = jnp.mean(o, axis=-1, keepdims=True)
    d = o - mu
    var = jnp.mean(d * d, axis=-1, keepdims=True)
    h = 0.5 * gate
    return (d * lax.rsqrt(var + EPS)).astype(BF16) * (h + h * jnp.tanh(h))


def _in_proj_kernel(x_ref, lnw_ref, w_ref, wlr_ref, p_ref, glr_ref, h_ref):
    @pl.when(pl.program_id(1) == 0)
    def _():
        h = (_rms(x_ref[...]) * lnw_ref[...]).astype(BF16)
        h_ref[...] = h
        glr_ref[...] = _dot_nt(h, wlr_ref[...]).astype(BF16)

    p_ref[...] = _dot(h_ref[...], w_ref[...]).astype(BF16)


def _regroup_kernel(wt_ref, o_ref):
    o_ref[...] = wt_ref[...].T.astype(BF16)


def _drop_columns(w, start, gap, tw):
    k, n = w.shape
    n_out = n - gap
    assert start % tw == 0 and n_out % tw == 0 and gap % BF16_ROWS == 0
    first_shifted = start // tw
    return pl.pallas_call(
        _regroup_kernel,
        grid=(n_out // tw,),
        in_specs=[pl.BlockSpec(
            (pl.Element(tw), pl.Element(k)),
            lambda j: (pl.multiple_of(j * tw + jnp.where(j >= first_shifted, gap, 0), BF16_ROWS),
                       0))],
        out_specs=pl.BlockSpec((k, tw), lambda j: (0, j)),
        out_shape=jax.ShapeDtypeStruct((k, n_out), BF16),
        compiler_params=pltpu.CompilerParams(
            dimension_semantics=("parallel",),
            vmem_limit_bytes=VMEM_LIMIT),
        name="w_in_regroup",
    )(w.T)


def _in_proj(x2, lnw, w_main, w_lr, tm, tn):
    t, d = x2.shape
    nc = w_main.shape[1]
    return pl.pallas_call(
        _in_proj_kernel,
        grid=(t // tm, nc // tn),
        in_specs=[
            pl.BlockSpec((tm, d), lambda i, j: (i, 0)),
            pl.BlockSpec((1, d), lambda i, j: (0, 0)),
            pl.BlockSpec((d, tn), lambda i, j: (0, j)),
            pl.BlockSpec((LANES, d), lambda i, j: (0, 0)),
        ],
        out_specs=[
            pl.BlockSpec((tm, tn), lambda i, j: (i, j)),
            pl.BlockSpec((tm, LANES), lambda i, j: (i, 0)),
        ],
        out_shape=[
            jax.ShapeDtypeStruct((t, nc), BF16),
            jax.ShapeDtypeStruct((t, LANES), BF16),
        ],
        scratch_shapes=[pltpu.VMEM((tm, d), BF16)],
        compiler_params=pltpu.CompilerParams(
            dimension_semantics=("parallel", "arbitrary"),
            vmem_limit_bytes=VMEM_LIMIT),
        name="in_proj",
    )(x2, lnw, w_main, w_lr)


def _chunk_cumsum(x, tri):
    hi = x.astype(BF16)
    r1 = x - hi.astype(F32)
    mid = r1.astype(BF16)
    lo = (r1 - mid.astype(F32)).astype(BF16)
    return _dot(tri, hi) + _dot(tri, mid) + _dot(tri, lo)


def _pad_rows(a, before, total):
    parts = []
    if before:
        parts.append(jnp.zeros((before, a.shape[1]), a.dtype))
    parts.append(a)
    after = total - before - a.shape[0]
    if after:
        parts.append(jnp.zeros((after, a.shape[1]), a.dtype))
    return jnp.concatenate(parts, axis=0) if len(parts) > 1 else a


def _gla_kernel(q_ref, k_ref, v_ref, glr0_ref, glr_next_ref, wdec_ref, bdec_ref, o_ref,
                state_ref, tri_ref, g_ref, *, dk, dv):
    tb = q_ref.shape[0]
    nsub = CHUNK // SUB
    scale = dk ** -0.5
    t = pl.program_id(1)

    def log_decay_prefix(glr):
        z = _dot(glr, wdec_ref[...]) + bdec_ref[...]
        log_a = (jnp.minimum(z, 0.0) - jnp.log1p(jnp.exp(-jnp.abs(z)))) * (1.0 / GLA_TAU)
        return _chunk_cumsum(log_a, tri_ref[...])

    @pl.when(t == 0)
    def _():
        state_ref[...] = jnp.zeros_like(state_ref)
        r = lax.broadcasted_iota(jnp.int32, (tb, tb), 0)
        c = lax.broadcasted_iota(jnp.int32, (tb, tb), 1)
        tri_ref[...] = jnp.where((c <= r) & (c // CHUNK == r // CHUNK), 1.0, 0.0).astype(BF16)
        g_ref[0] = log_decay_prefix(glr0_ref[...])

    row = lax.broadcasted_iota(jnp.int32, (CHUNK, CHUNK), 0)
    col = lax.broadcasted_iota(jnp.int32, (CHUNK, CHUNK), 1)
    lower = row >= col
    n_chunks = tb // CHUNK
    units = [(h, c) for c in range(n_chunks) for h in range(GLA_HEADS)]

    def rows(c):
        return slice(c * CHUNK, (c + 1) * CHUNK)

    def lanes(h, w):
        return slice(h * w, (h + 1) * w)


    prep = {}
    for h, c in units:
        g = g_ref[t % 2, rows(c), lanes(h, dk)]
        q = q_ref[rows(c), lanes(h, dk)].astype(F32) * scale
        k = k_ref[rows(c), lanes(h, dk)].astype(F32)
        refs = [g[i * SUB:i * SUB + 1, :] for i in range(nsub)]
        e = jnp.exp(g - jnp.concatenate(
            [jnp.broadcast_to(rf, (SUB, dk)) for rf in refs], axis=0))
        qe = (q * e).astype(BF16)
        ke = (k * e).astype(BF16)
        q_lo, k_lo, q_up, k_up = [], [], [], []
        for i in range(nsub):
            n = (i + 1) * SUB
            f = jnp.exp(refs[i] - g[:n, :])
            kf = k[:n] * f
            k_lo.append(_pad_rows(kf.astype(BF16), 0, CHUNK))
            q_up.append(_pad_rows((q[:n] * f).astype(BF16), 0, CHUNK))
            q_lo.append(_pad_rows(qe[i * SUB:n], i * SUB, CHUNK))
            k_up.append(_pad_rows(ke[i * SUB:n], i * SUB, CHUNK))
        g_last = g[CHUNK - 1:CHUNK, :]
        prep[h, c] = dict(
            q_lo=jnp.concatenate(q_lo, axis=1), k_lo=jnp.concatenate(k_lo, axis=1),
            q_up=jnp.concatenate(q_up, axis=1), k_up=jnp.concatenate(k_up, axis=1),
            q_in=(q * jnp.exp(g)).astype(BF16),
            k_dec=(kf * jnp.exp(g_last - refs[nsub - 1])).astype(BF16),
            decay=jnp.exp(g_last))

    scores, incr = {}, {}
    for h, c in units:
        p = prep[h, c]
        s_lo = _dot_nt(p["q_lo"], p["k_lo"])
        s_up = _dot_nt(p["q_up"], p["k_up"])
        scores[h, c] = jnp.where(lower, s_lo, s_up).astype(BF16)
        incr[h, c] = _dot_tn(p["k_dec"], v_ref[rows(c), lanes(h, dv)])

    g_next = log_decay_prefix(glr_next_ref[...])

    state_in = {}
    for h in range(GLA_HEADS):
        st = state_ref[h]
        for c in range(n_chunks):
            state_in[h, c] = st.astype(BF16)
            col = jnp.transpose(jnp.broadcast_to(prep[h, c]["decay"], (dk, dk)))
            st = st * jnp.concatenate([col] * (dv // dk), axis=1) + incr[h, c]
        state_ref[h] = st

    for h, c in units:
        o = _dot(jnp.concatenate([prep[h, c]["q_in"], scores[h, c]], axis=1),
                 jnp.concatenate([state_in[h, c], v_ref[rows(c), lanes(h, dv)]], axis=0))
        o_ref[rows(c), lanes(h, dv)] = o.astype(BF16)

    g_ref[(t + 1) % 2] = g_next


def _gla(p3, glr3, wdec, bdec, cols, tb):
    b, s, _ = p3.shape
    dk = wdec.shape[1] // GLA_HEADS
    dv = 2 * dk
    qk_w, v_w = GLA_HEADS * dk, GLA_HEADS * dv
    q_blk, k_blk, v_blk = cols["gq"] // qk_w, cols["gk"] // qk_w, cols["gv"] // v_w
    last = s // tb - 1
    return pl.pallas_call(
        functools.partial(_gla_kernel, dk=dk, dv=dv),
        grid=(b, s // tb),
        in_specs=[
            pl.BlockSpec((None, tb, qk_w), lambda i, t: (i, t, q_blk)),
            pl.BlockSpec((None, tb, qk_w), lambda i, t: (i, t, k_blk)),
            pl.BlockSpec((None, tb, v_w), lambda i, t: (i, t, v_blk)),
            pl.BlockSpec((None, tb, LANES), lambda i, t: (i, 0, 0)),
            pl.BlockSpec((None, tb, LANES), lambda i, t: (i, jnp.minimum(t + 1, last), 0)),
            pl.BlockSpec((LANES, qk_w), lambda i, t: (0, 0)),
            pl.BlockSpec((1, qk_w), lambda i, t: (0, 0)),
        ],
        out_specs=pl.BlockSpec((None, tb, v_w), lambda i, t: (i, t, 0)),
        out_shape=jax.ShapeDtypeStruct((b, s, v_w), BF16),
        scratch_shapes=[
            pltpu.VMEM((GLA_HEADS, dk, dv), F32),
            pltpu.VMEM((tb, tb), BF16),
            pltpu.VMEM((2, tb, qk_w), F32),
        ],
        compiler_params=pltpu.CompilerParams(
            dimension_semantics=("parallel", "arbitrary"),
            vmem_limit_bytes=VMEM_LIMIT),
        name="gla",
    )(p3, p3, p3, glr3, glr3, wdec, bdec)


def _ret_kernel(pos0_ref, q_ref, k_ref, *rest, dk, dv, n_parts, tile):
    v_refs = rest[:n_parts]
    freq_ref, o_ref, state_ref, decay_ref, qk_decay_ref, rot_ref = rest[n_parts:]
    n_sub = q_ref.shape[0] // tile
    half = dk // 2
    heads_per_part = RET_HEADS // n_parts

    @pl.when(pl.program_id(1) == 0)
    def _():
        state_ref[...] = jnp.zeros_like(state_ref)
        row = lax.broadcasted_iota(jnp.int32, (tile, tile), 0)
        col = lax.broadcasted_iota(jnp.int32, (tile, tile), 1)
        visible = (col // CHUNK) <= (row // CHUNK)
        dist = jnp.abs(row - col).astype(F32)
        idx = lax.broadcasted_iota(jnp.int32, (tile, LANES), 0).astype(F32)
        for h in range(RET_HEADS):
            log_gamma = math.log(1.0 - 2.0 ** (-5.0 - h))
            decay_ref[h] = jnp.where(visible, jnp.exp(log_gamma * dist), 0.0) * (dk ** -0.5)
            qk_decay_ref[h, 0] = jnp.exp(log_gamma * (idx + 1.0))
            qk_decay_ref[h, 1] = jnp.exp(log_gamma * (tile - 1.0 - idx)) * (dk ** -0.5)
        ang = lax.broadcasted_iota(jnp.int32, (tile, half), 0).astype(F32) * freq_ref[...]
        rot_ref[0] = jnp.cos(ang)
        rot_ref[1] = jnp.sin(ang)

    def widen(tbl):
        return jnp.concatenate([tbl] * (dk // LANES), axis=1)

    states = [state_ref[h] for h in range(RET_HEADS)]
    for sub in range(n_sub):
        rs = slice(sub * tile, (sub + 1) * tile)
        pos0 = pos0_ref[pl.program_id(0), pl.program_id(1) * n_sub + sub]
        ang0 = pos0.astype(F32) * freq_ref[...]
        c0, s0 = jnp.cos(ang0), jnp.sin(ang0)
        cr, sr = rot_ref[0], rot_ref[1]
        cos = c0 * cr - s0 * sr
        sin = s0 * cr + c0 * sr

        def rope(x):
            x1, x2 = x[:, :half], x[:, half:]
            return jnp.concatenate([x1 * cos - x2 * sin, x1 * sin + x2 * cos], axis=1)

        for h in range(RET_HEADS):
            log_gamma = math.log(1.0 - 2.0 ** (-5.0 - h))
            ks = slice(h * dk, (h + 1) * dk)
            part, vs = h // heads_per_part, slice((h % heads_per_part) * dv,
                                                  (h % heads_per_part + 1) * dv)
            q = rope(q_ref[rs, ks].astype(F32))
            k = rope(k_ref[rs, ks].astype(F32))
            v = v_refs[part][rs, vs]
            st = states[h]

            s = (_dot_nt(q.astype(BF16), k.astype(BF16)) * decay_ref[h]).astype(BF16)
            o = _dot(s, v) + _dot((q * widen(qk_decay_ref[h, 0])).astype(BF16),
                                  st.astype(BF16))
            states[h] = (math.exp(log_gamma * tile) * st
                         + _dot_tn((k * widen(qk_decay_ref[h, 1])).astype(BF16), v))
            o_ref[rs, h * dv:(h + 1) * dv] = o.astype(BF16)
    for h in range(RET_HEADS):
        state_ref[h] = states[h]


def _retention(p3, pos0, inv_freq, cols, dk, dv, tile, tb):
    b, s, _ = p3.shape
    qk_w, v_w = RET_HEADS * dk, RET_HEADS * dv
    q_blk, k_blk = cols["rq"] // qk_w, cols["rk"] // qk_w
    part_w = math.gcd(cols["rv"], v_w)
    n_parts = v_w // part_w
    assert part_w % dv == 0 and tb % tile == 0

    def part_spec(off, j):
        blk = off // part_w + j
        return pl.BlockSpec((None, tb, part_w), lambda i, t, pos: (i, t, blk))

    grid_spec = pltpu.PrefetchScalarGridSpec(
        num_scalar_prefetch=1,
        grid=(b, s // tb),
        in_specs=[
            pl.BlockSpec((None, tb, qk_w), lambda i, t, pos: (i, t, q_blk)),
            pl.BlockSpec((None, tb, qk_w), lambda i, t, pos: (i, t, k_blk)),
            *[part_spec(cols["rv"], j) for j in range(n_parts)],
            pl.BlockSpec((1, dk // 2), lambda i, t, pos: (0, 0)),
        ],
        out_specs=pl.BlockSpec((None, tb, v_w), lambda i, t, pos: (i, t, 0)),
        scratch_shapes=[
            pltpu.VMEM((RET_HEADS, dk, dv), F32),
            pltpu.VMEM((RET_HEADS, tile, tile), F32),
            pltpu.VMEM((RET_HEADS, 2, tile, LANES), F32),
            pltpu.VMEM((2, tile, dk // 2), F32),
        ],
    )
    return pl.pallas_call(
        functools.partial(_ret_kernel, dk=dk, dv=dv, n_parts=n_parts, tile=tile),
        grid_spec=grid_spec,
        out_shape=jax.ShapeDtypeStruct((b, s, v_w), BF16),
        compiler_params=pltpu.CompilerParams(
            dimension_semantics=("parallel", "arbitrary"),
            vmem_limit_bytes=VMEM_LIMIT),
        name="retention",
    )(pos0, p3, p3, *([p3] * n_parts), inv_freq)


def _mix_kernel(*refs, gla_dv, ret_dv, n_rg):
    og_ref, or_ref, gr_ref = refs[:3]
    rg_refs = refs[3:3 + n_rg]
    (ag_ref, ar_ref, x_ref, wgo_ref, wro_ref, wmo_ref,
     ln_post_ref, ln_pre_ref, x1_ref, h2_ref) = refs[3 + n_rg:]

    def normed(o_ref, gate_refs, dv, rs):
        heads_per_ref = gate_refs[0].shape[1] // dv
        outs = []
        for h in range(o_ref.shape[1] // dv):
            gate = gate_refs[h // heads_per_ref][rs, (h % heads_per_ref) * dv:
                                                 (h % heads_per_ref + 1) * dv]
            outs.append(_head_norm_gate(o_ref[rs, h * dv:(h + 1) * dv].astype(F32), gate))
        return jnp.concatenate(outs, axis=1)

    tm = x_ref.shape[0]
    for r in range(tm // MIX_ROWS):
        rs = slice(r * MIX_ROWS, (r + 1) * MIX_ROWS)
        y_gla = _dot(normed(og_ref, (gr_ref,), gla_dv, rs), wgo_ref[...])
        y_ret = _dot(normed(or_ref, rg_refs, ret_dv, rs), wro_ref[...])
        merged = (_sigmoid(ag_ref[rs, :]) * y_gla.astype(BF16)
                  + _sigmoid(ar_ref[rs, :]) * y_ret.astype(BF16))
        mo = _dot(merged, wmo_ref[...])
        x1 = x_ref[rs, :] + _rms(mo) * ln_post_ref[...]
        x1_ref[rs, :] = x1
        h2_ref[rs, :] = (_rms(x1) * ln_pre_ref[...]).astype(BF16)


def _resident(shape):
    return pl.BlockSpec(shape, lambda i: (0,) * len(shape), pipeline_mode=pl.Buffered(1))


def _mix_out(og, orr, p, x2, wgo, wro, wmo, ln_post, ln_pre, cols, tm):
    t, d = x2.shape
    gla_v, ret_v = og.shape[1], orr.shape[1]
    rg_w = math.gcd(cols["rg"], ret_v)
    n_rg = ret_v // rg_w

    def col_spec(off, w):
        blk = off // w
        return pl.BlockSpec((tm, w), lambda i: (i, blk))

    return pl.pallas_call(
        functools.partial(_mix_kernel, gla_dv=gla_v // GLA_HEADS, ret_dv=ret_v // RET_HEADS,
                          n_rg=n_rg),
        grid=(t // tm,),
        in_specs=[
            pl.BlockSpec((tm, gla_v), lambda i: (i, 0)),
            pl.BlockSpec((tm, ret_v), lambda i: (i, 0)),
            col_spec(cols["gr"], gla_v),
            *[col_spec(cols["rg"] + j * rg_w, rg_w) for j in range(n_rg)],
            col_spec(cols["ag"], d),
            col_spec(cols["ar"], d),
            pl.BlockSpec((tm, d), lambda i: (i, 0)),
            _resident(wgo.shape),
            _resident(wro.shape),
            _resident(wmo.shape),
            _resident((1, d)),
            _resident((1, d)),
        ],
        out_specs=[
            pl.BlockSpec((tm, d), lambda i: (i, 0)),
            pl.BlockSpec((tm, d), lambda i: (i, 0)),
        ],
        out_shape=[
            jax.ShapeDtypeStruct((t, d), F32),
            jax.ShapeDtypeStruct((t, d), BF16),
        ],
        compiler_params=pltpu.CompilerParams(
            dimension_semantics=("parallel",),
            vmem_limit_bytes=VMEM_LIMIT),
        name="mix_out",
    )(og, orr, p, *([p] * n_rg), p, p, x2, wgo, wro, wmo, ln_post, ln_pre)


_GELU_C = math.sqrt(2.0 / math.pi)


def _ffn_kernel(h_ref, halo_ref, x1_ref, wup_ref, cw_ref, cb_ref, wdn_ref, ln_ref, o_ref,
                u0_ref, u1_ref, u2_ref, u3_ref, f_ref, z_ref, *, tiles_per_seq, d_ff, tn):
    u_ref = (u0_ref, u1_ref, u2_ref, u3_ref)
    tm, d = o_ref.shape
    n_phase = CONV_PHASES
    rows_pp = tm // n_phase
    first = (pl.program_id(0) % tiles_per_seq) == 0
    halo = jnp.where(first, jnp.zeros_like(halo_ref[...]), halo_ref[...])
    h_ext = jnp.concatenate([halo, h_ref[...]], axis=0)

    def up(c):
        for half in range(2):
            c0 = half * d_ff + c * tn
            u = _dot(h_ext, wup_ref[:, c0:c0 + tn])
            for l in range(tn // LANES):
                u_ref[2 * (c % 2) + half][l] = u[:, l * LANES:(l + 1) * LANES]

    def conv(slot, c0, gain):
        slabs = []
        for l in range(tn // LANES):
            cl = slice(c0 + l * LANES, c0 + (l + 1) * LANES)
            taps = [cw_ref[j:j + 1, cl] * gain for j in range(CONV_W)]
            bias = cb_ref[:, cl] * gain
            phases = []
            for s in range(n_phase):
                y = bias
                for j in range(CONV_W):
                    r0 = BF16_ROWS - (CONV_W - 1) + j + s
                    y = y + u_ref[slot][l, pl.ds(r0, rows_pp, stride=n_phase), :] * taps[j]
                phases.append(y)
            slabs.append(jnp.concatenate(phases, axis=0))
        return jnp.concatenate(slabs, axis=1)

    n_tiles = d_ff // tn
    up(0)
    for c in range(n_tiles):
        if c + 1 < n_tiles:
            up(c + 1)
        half_val = conv(2 * (c % 2), c * tn, 0.5)
        g = conv(2 * (c % 2) + 1, d_ff + c * tn, 1.0)
        t = jnp.tanh(g * (_GELU_C + (_GELU_C * 0.044715) * (g * g)))
        f_ref[:, c * tn:(c + 1) * tn] = ((g + g * t) * half_val).astype(BF16)

    z = _rms(_dot(f_ref[...], wdn_ref[...])) * ln_ref[...]
    for l in range(d // LANES):
        for s in range(n_phase):
            z_ref[l, pl.ds(s, rows_pp, stride=n_phase), :] = (
                z[s * rows_pp:(s + 1) * rows_pp, l * LANES:(l + 1) * LANES])
    o_ref[...] = x1_ref[...] + jnp.concatenate([z_ref[l] for l in range(d // LANES)], axis=1)


def _conv_ffn(h2, x1, wup, cw, cb, wdn, ln, seq, tm, tn):
    t, d = x1.shape
    d_ff = wdn.shape[0]
    halo_blocks = tm // BF16_ROWS
    return pl.pallas_call(
        functools.partial(_ffn_kernel, tiles_per_seq=seq // tm, d_ff=d_ff, tn=tn),
        grid=(t // tm,),
        in_specs=[
            pl.BlockSpec((tm, d), lambda i: (i, 0)),
            pl.BlockSpec((BF16_ROWS, d), lambda i: (jnp.maximum(i * halo_blocks - 1, 0), 0)),
            pl.BlockSpec((tm, d), lambda i: (i, 0)),
            _resident(wup.shape),
            _resident(cw.shape),
            _resident(cb.shape),
            _resident(wdn.shape),
            _resident((1, d)),
        ],
        out_specs=pl.BlockSpec((tm, d), lambda i: (i, 0)),
        out_shape=jax.ShapeDtypeStruct((t, d), F32),
        scratch_shapes=[pltpu.VMEM((tn // LANES, BF16_ROWS + tm, LANES), F32)] * 4
        + [pltpu.VMEM((tm, d_ff), BF16), pltpu.VMEM((d // LANES, tm, LANES), F32)],
        compiler_params=pltpu.CompilerParams(
            dimension_semantics=("parallel",),
            vmem_limit_bytes=VMEM_LIMIT),
        name="conv_ffn",
    )(h2, h2, x1, wup, cw, cb, wdn, ln)


def _largest_tile(n, cap, quantum):
    best = quantum
    for cand in range(quantum, min(n, cap) + 1, quantum):
        if n % cand == 0:
            best = cand
    return best


def kernel(x, positions, ln_pre_mix, w_in, w_gla_decay, b_gla_decay, w_gla_out, w_ret_out,
           w_mix_out, ln_post_mix, ln_pre_ffn, w_ffn_up, conv_w, conv_b, w_ffn_down, ln_post_ffn):
    b, s, d = x.shape
    depth = w_in.shape[0]
    gla_qk = w_gla_decay.shape[2]
    gla_v = w_gla_out.shape[1]
    ret_v = w_ret_out.shape[1]
    ret_qk = ret_v // 2
    d_ff = w_ffn_down.shape[1]
    ret_dk, ret_dv = ret_qk // RET_HEADS, ret_v // RET_HEADS

    names = ("gq", "gk", "gv", "gr", "glr", "rq", "rk", "rv", "rg", "ag", "ar")
    widths = (gla_qk, gla_qk, gla_v, gla_v, GLA_GATE_RANK, ret_qk, ret_qk, ret_v, ret_v, d, d)
    lr0 = sum(widths[:names.index("glr")])
    cols, off = {}, 0
    for nm, w in zip(names, widths):
        if nm != "glr":
            cols[nm] = off
            off += w
    n_main = off

    inv_freq = (ROPE_BASE ** (-jnp.arange(0, ret_dk, 2, dtype=F32) / ret_dk))[None, :]

    tm_in = _largest_tile(b * s, 1024, 256)
    tn_in = _largest_tile(n_main, 2816, 256)
    tb = _largest_tile(s, 256, CHUNK)
    tb_ret = _largest_tile(s, 2 * tb, tb)
    tm_mix = _largest_tile(s, 512, 256)
    tm_ffn = _largest_tile(s, 512, 256)
    tn_ffn = _largest_tile(d_ff, 256, 256)
    pos0 = positions[:, ::tb]

    x2 = x.reshape(b * s, d)
    for layer in range(depth):
        w = w_in[layer]
        w_main = _drop_columns(w, lr0, GLA_GATE_RANK, _largest_tile(n_main, 1024, 256))
        w_lr = jnp.pad(w.T[lr0:lr0 + GLA_GATE_RANK],
                       ((0, LANES - GLA_GATE_RANK), (0, 0))).astype(BF16)
        wdec = jnp.pad(w_gla_decay[layer], ((0, LANES - GLA_GATE_RANK), (0, 0))).astype(BF16)
        bdec = b_gla_decay[layer][None, :]

        p, glr = _in_proj(x2, ln_pre_mix[layer][None, :], w_main, w_lr, tm_in, tn_in)
        p3 = p.reshape(b, s, -1)
        og = _gla(p3, glr.reshape(b, s, LANES), wdec, bdec, cols, tb)
        orr = _retention(p3, pos0, inv_freq, cols, ret_dk, ret_dv, tb, tb_ret)
        x1, h2 = _mix_out(og.reshape(b * s, gla_v), orr.reshape(b * s, ret_v), p, x2,
                          w_gla_out[layer].astype(BF16), w_ret_out[layer].astype(BF16),
                          w_mix_out[layer].astype(BF16), ln_post_mix[layer][None, :],
                          ln_pre_ffn[layer][None, :], cols, tm_mix)
        x2 = _conv_ffn(h2, x1, w_ffn_up[layer].astype(BF16), conv_w[layer],
                       conv_b[layer][None, :], w_ffn_down[layer].astype(BF16),
                       ln_post_ffn[layer][None, :], s, tm_ffn, tn_ffn)
    return x2.reshape(b, s, d)
```

```python
import functools
import math

import jax
import jax.numpy as jnp
from jax import lax
from jax.experimental import pallas as pl
from jax.experimental.pallas import tpu as pltpu

F32 = jnp.float32
BF16 = jnp.bfloat16

CHUNK = 64
SUB = 16
EPS = 1e-6
GLA_HEADS = 4
GLA_GATE_RANK = 16
GLA_TAU = 16.0
RET_HEADS = 4
ROPE_BASE = 10000.0
CONV_W = 3
LANES = 128
BF16_ROWS = 16
MIX_ROWS = 256
IN_ROWS = 512
CONV_PHASES = 4

VMEM_LIMIT = 56 * 1024 * 1024


def _rms(x):
    return x * lax.rsqrt(jnp.mean(x * x, axis=-1, keepdims=True) + EPS)


def _sigmoid(x):
    return 0.5 + 0.5 * jnp.tanh(0.5 * x)


def _dot(a, b):
    return jnp.dot(a, b, preferred_element_type=F32)


def _dot_nt(a, b):
    return lax.dot_general(a, b, (((1,), (1,)), ((), ())), preferred_element_type=F32)


def _dot_tn(a, b):
    return lax.dot_general(a, b, (((0,), (0,)), ((), ())), preferred_element_type=F32)


def _head_norm_gate(o, gate):
    mu = jnp.mean(o, axis=-1, keepdims=True)
    d = o - mu
    var = jnp.mean(d * d, axis=-1, keepdims=True)
    h = 0.5 * gate
    return (d * lax.rsqrt(var + EPS)).astype(BF16) * (h + h * jnp.tanh(h))


def _in_proj_kernel(x_ref, lnw_ref, w_ref, wlr_ref, p_ref, glr_ref, h_ref):
    j = pl.program_id(1)

    @pl.when(j == 0)
    def _():
        for r in range(x_ref.shape[0] // IN_ROWS):
            rs = slice(r * IN_ROWS, (r + 1) * IN_ROWS)
            h = (_rms(x_ref[rs, :]) * lnw_ref[...]).astype(BF16)
            h_ref[rs, :] = h
            glr_ref[rs, :] = _dot_nt(h, wlr_ref[...]).astype(BF16)
            p_ref[rs, :] = _dot(h, w_ref[...]).astype(BF16)

    @pl.when(j != 0)
    def _():
        p_ref[...] = _dot(h_ref[...], w_ref[...]).astype(BF16)


def _regroup_kernel(wt_ref, o_ref):
    o_ref[...] = wt_ref[...].T.astype(BF16)


def _drop_columns(w, start, gap, tw):
    k, n = w.shape
    n_out = n - gap
    assert start % tw == 0 and n_out % tw == 0 and gap % BF16_ROWS == 0
    first_shifted = start // tw
    return pl.pallas_call(
        _regroup_kernel,
        grid=(n_out // tw,),
        in_specs=[pl.BlockSpec(
            (pl.Element(tw), pl.Element(k)),
            lambda j: (pl.multiple_of(j * tw + jnp.where(j >= first_shifted, gap, 0), BF16_ROWS),
                       0))],
        out_specs=pl.BlockSpec((k, tw), lambda j: (0, j)),
        out_shape=jax.ShapeDtypeStruct((k, n_out), BF16),
        compiler_params=pltpu.CompilerParams(
            dimension_semantics=("parallel",),
            vmem_limit_bytes=VMEM_LIMIT),
        name="w_in_regroup",
    )(w.T)


def _in_proj(x2, lnw, w_main, w_lr, tm, tn):
    t, d = x2.shape
    nc = w_main.shape[1]
    return pl.pallas_call(
        _in_proj_kernel,
        grid=(t // tm, nc // tn),
        in_specs=[
            pl.BlockSpec((tm, d), lambda i, j: (i, 0)),
            pl.BlockSpec((1, d), lambda i, j: (0, 0)),
            pl.BlockSpec((d, tn), lambda i, j: (0, j)),
            pl.BlockSpec((LANES, d), lambda i, j: (0, 0)),
        ],
        out_specs=[
            pl.BlockSpec((tm, tn), lambda i, j: (i, j)),
            pl.BlockSpec((tm, LANES), lambda i, j: (i, 0)),
        ],
        out_shape=[
            jax.ShapeDtypeStruct((t, nc), BF16),
            jax.ShapeDtypeStruct((t, LANES), BF16),
        ],
        scratch_shapes=[pltpu.VMEM((tm, d), BF16)],
        compiler_params=pltpu.CompilerParams(
            dimension_semantics=("parallel", "arbitrary"),
            vmem_limit_bytes=VMEM_LIMIT),
        name="in_proj",
    )(x2, lnw, w_main, w_lr)


def _chunk_cumsum(x, tri):
    hi = x.astype(BF16)
    r1 = x - hi.astype(F32)
    mid = r1.astype(BF16)
    lo = (r1 - mid.astype(F32)).astype(BF16)
    return _dot(tri, hi) + _dot(tri, mid) + _dot(tri, lo)


def _pad_rows(a, before, total):
    parts = []
    if before:
        parts.append(jnp.zeros((before, a.shape[1]), a.dtype))
    parts.append(a)
    after = total - before - a.shape[0]
    if after:
        parts.append(jnp.zeros((after, a.shape[1]), a.dtype))
    return jnp.concatenate(parts, axis=0) if len(parts) > 1 else a


def _log_decay_prefix(glr, wdec_ref, bdec_ref, tri_ref):
    z = _dot(glr, wdec_ref[...]) + bdec_ref[...]
    log_a = (jnp.minimum(z, 0.0) - jnp.log1p(jnp.exp(-jnp.abs(z)))) * (1.0 / GLA_TAU)
    return _chunk_cumsum(log_a, tri_ref[...])


def _gla_init(glr0_ref, wdec_ref, bdec_ref, state_ref, tri_ref, g_ref):
    tb = tri_ref.shape[0]
    state_ref[...] = jnp.zeros_like(state_ref)
    r = lax.broadcasted_iota(jnp.int32, (tb, tb), 0)
    c = lax.broadcasted_iota(jnp.int32, (tb, tb), 1)
    tri_ref[...] = jnp.where((c <= r) & (c // CHUNK == r // CHUNK), 1.0, 0.0).astype(BF16)
    g_ref[0] = _log_decay_prefix(glr0_ref[...], wdec_ref, bdec_ref, tri_ref)


def _gla_stages(t, q_ref, k_ref, v_ref, glr_next_ref, wdec_ref, bdec_ref, o_ref,
                state_ref, tri_ref, g_ref, dk, dv):
    tb = q_ref.shape[0]
    nsub = CHUNK // SUB
    scale = dk ** -0.5
    row = lax.broadcasted_iota(jnp.int32, (CHUNK, CHUNK), 0)
    col = lax.broadcasted_iota(jnp.int32, (CHUNK, CHUNK), 1)
    lower = row >= col
    n_chunks = tb // CHUNK
    units = [(h, c) for c in range(n_chunks) for h in range(GLA_HEADS)]

    def rows(c):
        return slice(c * CHUNK, (c + 1) * CHUNK)

    def lanes(h, w):
        return slice(h * w, (h + 1) * w)

    prep = {}
    for h, c in units:
        g = g_ref[t % 2, rows(c), lanes(h, dk)]
        q = q_ref[rows(c), lanes(h, dk)].astype(F32) * scale
        k = k_ref[rows(c), lanes(h, dk)].astype(F32)
        refs = [g[i * SUB:i * SUB + 1, :] for i in range(nsub)]
        e = jnp.exp(g - jnp.concatenate(
            [jnp.broadcast_to(rf, (SUB, dk)) for rf in refs], axis=0))
        qe = (q * e).astype(BF16)
        ke = (k * e).astype(BF16)
        q_lo, k_lo, q_up, k_up = [], [], [], []
        for i in range(nsub):
            n = (i + 1) * SUB
            f = jnp.exp(refs[i] - g[:n, :])
            kf = k[:n] * f
            k_lo.append(_pad_rows(kf.astype(BF16), 0, CHUNK))
            q_up.append(_pad_rows((q[:n] * f).astype(BF16), 0, CHUNK))
            q_lo.append(_pad_rows(qe[i * SUB:n], i * SUB, CHUNK))
            k_up.append(_pad_rows(ke[i * SUB:n], i * SUB, CHUNK))
        g_last = g[CHUNK - 1:CHUNK, :]
        prep[h, c] = dict(
            q_lo=jnp.concatenate(q_lo, axis=1), k_lo=jnp.concatenate(k_lo, axis=1),
            q_up=jnp.concatenate(q_up, axis=1), k_up=jnp.concatenate(k_up, axis=1),
            q_in=(q * jnp.exp(g)).astype(BF16),
            k_dec=(kf * jnp.exp(g_last - refs[nsub - 1])).astype(BF16),
            decay=jnp.exp(g_last))
    yield

    scores, incr = {}, {}
    for h, c in units:
        p = prep[h, c]
        s_lo = _dot_nt(p["q_lo"], p["k_lo"])
        s_up = _dot_nt(p["q_up"], p["k_up"])
        scores[h, c] = jnp.where(lower, s_lo, s_up).astype(BF16)
        incr[h, c] = _dot_tn(p["k_dec"], v_ref[rows(c), lanes(h, dv)])
    yield

    g_next = _log_decay_prefix(glr_next_ref[...], wdec_ref, bdec_ref, tri_ref)
    yield

    state_in = {}
    for h in range(GLA_HEADS):
        st = state_ref[h]
        for c in range(n_chunks):
            state_in[h, c] = st.astype(BF16)
            col_decay = jnp.transpose(jnp.broadcast_to(prep[h, c]["decay"], (dk, dk)))
            st = st * jnp.concatenate([col_decay] * (dv // dk), axis=1) + incr[h, c]
        state_ref[h] = st
    yield

    for h, c in units:
        o = _dot(jnp.concatenate([prep[h, c]["q_in"], scores[h, c]], axis=1),
                 jnp.concatenate([state_in[h, c], v_ref[rows(c), lanes(h, dv)]], axis=0))
        o_ref[rows(c), lanes(h, dv)] = o.astype(BF16)
    g_ref[(t + 1) % 2] = g_next


def _ret_init(freq_ref, state_ref, decay_ref, qk_decay_ref, rot_ref, dk):
    tile = decay_ref.shape[1]
    state_ref[...] = jnp.zeros_like(state_ref)
    row = lax.broadcasted_iota(jnp.int32, (tile, tile), 0)
    col = lax.broadcasted_iota(jnp.int32, (tile, tile), 1)
    visible = (col // CHUNK) <= (row // CHUNK)
    dist = jnp.abs(row - col).astype(F32)
    idx = lax.broadcasted_iota(jnp.int32, (tile, LANES), 0).astype(F32)
    for h in range(RET_HEADS):
        log_gamma = math.log(1.0 - 2.0 ** (-5.0 - h))
        decay_ref[h] = jnp.where(visible, jnp.exp(log_gamma * dist), 0.0) * (dk ** -0.5)
        qk_decay_ref[h, 0] = jnp.exp(log_gamma * (idx + 1.0))
        qk_decay_ref[h, 1] = jnp.exp(log_gamma * (tile - 1.0 - idx)) * (dk ** -0.5)
    ang = lax.broadcasted_iota(jnp.int32, (tile, dk // 2), 0).astype(F32) * freq_ref[...]
    rot_ref[0] = jnp.cos(ang)
    rot_ref[1] = jnp.sin(ang)


def _ret_stages(pos0, q_ref, k_ref, v_refs, freq_ref, o_ref,
                state_ref, decay_ref, qk_decay_ref, rot_ref, dk, dv):
    tile = q_ref.shape[0]
    half = dk // 2
    heads_per_part = RET_HEADS // len(v_refs)

    ang0 = pos0.astype(F32) * freq_ref[...]
    c0, s0 = jnp.cos(ang0), jnp.sin(ang0)
    cr, sr = rot_ref[0], rot_ref[1]
    cos = c0 * cr - s0 * sr
    sin = s0 * cr + c0 * sr

    def rope(x):
        x1, x2 = x[:, :half], x[:, half:]
        return jnp.concatenate([x1 * cos - x2 * sin, x1 * sin + x2 * cos], axis=1)

    def widen(tbl):
        return jnp.concatenate([tbl] * (dk // LANES), axis=1)

    heads = range(RET_HEADS)
    log_gamma = [math.log(1.0 - 2.0 ** (-5.0 - h)) for h in heads]

    def v_of(h):
        return v_refs[h // heads_per_part][:, (h % heads_per_part) * dv:
                                           (h % heads_per_part + 1) * dv]

    q = [rope(q_ref[:, h * dk:(h + 1) * dk].astype(F32)) for h in heads]
    k = [rope(k_ref[:, h * dk:(h + 1) * dk].astype(F32)) for h in heads]
    qb = [x.astype(BF16) for x in q]
    kb = [x.astype(BF16) for x in k]
    yield
    s = [_dot_nt(qb[h], kb[h]) for h in heads]
    yield
    sb = [(s[h] * decay_ref[h]).astype(BF16) for h in heads]
    q_in = [(q[h] * widen(qk_decay_ref[h, 0])).astype(BF16) for h in heads]
    k_in = [(k[h] * widen(qk_decay_ref[h, 1])).astype(BF16) for h in heads]
    st_b = [state_ref[h].astype(BF16) for h in heads]
    yield
    o = [_dot(sb[h], v_of(h)) + _dot(q_in[h], st_b[h]) for h in heads]
    incr = [_dot_tn(k_in[h], v_of(h)) for h in heads]
    yield
    for h in heads:
        state_ref[h] = math.exp(log_gamma[h] * tile) * state_ref[h] + incr[h]
        o_ref[:, h * dv:(h + 1) * dv] = o[h].astype(BF16)


def _mixers_kernel(pos0_ref, gq_ref, gk_ref, gv_ref, glr0_ref, glr_next_ref, wdec_ref, bdec_ref,
                   rq_ref, rk_ref, *rest, gla_dk, gla_dv, ret_dk, ret_dv, n_parts):
    rv_refs = rest[:n_parts]
    (freq_ref, og_ref, or_ref, gla_state_ref, tri_ref, g_ref,
     ret_state_ref, decay_ref, qk_decay_ref, rot_ref) = rest[n_parts:]
    t = pl.program_id(1)

    @pl.when(t == 0)
    def _():
        _gla_init(glr0_ref, wdec_ref, bdec_ref, gla_state_ref, tri_ref, g_ref)
        _ret_init(freq_ref, ret_state_ref, decay_ref, qk_decay_ref, rot_ref, ret_dk)

    gla = _gla_stages(t, gq_ref, gk_ref, gv_ref, glr_next_ref, wdec_ref, bdec_ref, og_ref,
                      gla_state_ref, tri_ref, g_ref, gla_dk, gla_dv)
    ret = _ret_stages(pos0_ref[pl.program_id(0), t], rq_ref, rk_ref, rv_refs, freq_ref, or_ref,
                      ret_state_ref, decay_ref, qk_decay_ref, rot_ref, ret_dk, ret_dv)
    for which in "rrgrrggrgg":
        next(ret if which == "r" else gla, None)


def _mixers(p3, glr3, wdec, bdec, pos0, inv_freq, cols, ret_dk, ret_dv, tb):
    b, s, _ = p3.shape
    gla_dk = wdec.shape[1] // GLA_HEADS
    gla_dv = 2 * gla_dk
    gqk_w, gv_w = GLA_HEADS * gla_dk, GLA_HEADS * gla_dv
    rqk_w, rv_w = RET_HEADS * ret_dk, RET_HEADS * ret_dv
    part_w = math.gcd(cols["rv"], rv_w)
    n_parts = rv_w // part_w
    assert part_w % ret_dv == 0
    last = s // tb - 1

    def col_spec(off, w):
        blk = off // w
        return pl.BlockSpec((None, tb, w), lambda i, t, pos: (i, t, blk))

    grid_spec = pltpu.PrefetchScalarGridSpec(
        num_scalar_prefetch=1,
        grid=(b, s // tb),
        in_specs=[
            col_spec(cols["gq"], gqk_w),
            col_spec(cols["gk"], gqk_w),
            col_spec(cols["gv"], gv_w),
            pl.BlockSpec((None, tb, LANES), lambda i, t, pos: (i, 0, 0)),
            pl.BlockSpec((None, tb, LANES), lambda i, t, pos: (i, jnp.minimum(t + 1, last), 0)),
            pl.BlockSpec((LANES, gqk_w), lambda i, t, pos: (0, 0)),
            pl.BlockSpec((1, gqk_w), lambda i, t, pos: (0, 0)),
            col_spec(cols["rq"], rqk_w),
            col_spec(cols["rk"], rqk_w),
            *[col_spec(cols["rv"] + j * part_w, part_w) for j in range(n_parts)],
            pl.BlockSpec((1, ret_dk // 2), lambda i, t, pos: (0, 0)),
        ],
        out_specs=[
            pl.BlockSpec((None, tb, gv_w), lambda i, t, pos: (i, t, 0)),
            pl.BlockSpec((None, tb, rv_w), lambda i, t, pos: (i, t, 0)),
        ],
        scratch_shapes=[
            pltpu.VMEM((GLA_HEADS, gla_dk, gla_dv), F32),
            pltpu.VMEM((tb, tb), BF16),
            pltpu.VMEM((2, tb, gqk_w), F32),
            pltpu.VMEM((RET_HEADS, ret_dk, ret_dv), F32),
            pltpu.VMEM((RET_HEADS, tb, tb), F32),
            pltpu.VMEM((RET_HEADS, 2, tb, LANES), F32),
            pltpu.VMEM((2, tb, ret_dk // 2), F32),
        ],
    )
    return pl.pallas_call(
        functools.partial(_mixers_kernel, gla_dk=gla_dk, gla_dv=gla_dv, ret_dk=ret_dk,
                          ret_dv=ret_dv, n_parts=n_parts),
        grid_spec=grid_spec,
        out_shape=[
            jax.ShapeDtypeStruct((b, s, gv_w), BF16),
            jax.ShapeDtypeStruct((b, s, rv_w), BF16),
        ],
        compiler_params=pltpu.CompilerParams(
            dimension_semantics=("parallel", "arbitrary"),
            vmem_limit_bytes=VMEM_LIMIT),
        name="mixers",
    )(pos0, p3, p3, p3, glr3, glr3, wdec, bdec, p3, p3, *([p3] * n_parts), inv_freq)


def _mix_kernel(*refs, gla_dv, ret_dv, n_rg):
    og_ref, or_ref, gr_ref = refs[:3]
    rg_refs = refs[3:3 + n_rg]
    (ag_ref, ar_ref, x_ref, wgo_ref, wro_ref, wmo_ref,
     ln_post_ref, ln_pre_ref, x1_ref, h2_ref) = refs[3 + n_rg:]

    def normed(o_ref, gate_refs, dv, rs):
        heads_per_ref = gate_refs[0].shape[1] // dv
        outs = []
        for h in range(o_ref.shape[1] // dv):
            gate = gate_refs[h // heads_per_ref][rs, (h % heads_per_ref) * dv:
                                                 (h % heads_per_ref + 1) * dv]
            outs.append(_head_norm_gate(o_ref[rs, h * dv:(h + 1) * dv].astype(F32), gate))
        return jnp.concatenate(outs, axis=1)

    tm = x_ref.shape[0]
    for r in range(tm // MIX_ROWS):
        rs = slice(r * MIX_ROWS, (r + 1) * MIX_ROWS)
        y_gla = _dot(normed(og_ref, (gr_ref,), gla_dv, rs), wgo_ref[...])
        y_ret = _dot(normed(or_ref, rg_refs, ret_dv, rs), wro_ref[...])
        merged = (_sigmoid(ag_ref[rs, :]) * y_gla.astype(BF16)
                  + _sigmoid(ar_ref[rs, :]) * y_ret.astype(BF16))
        mo = _dot(merged, wmo_ref[...])
        x1 = x_ref[rs, :] + _rms(mo) * ln_post_ref[...]
        x1_ref[rs, :] = x1
        h2_ref[rs, :] = (_rms(x1) * ln_pre_ref[...]).astype(BF16)


def _resident(shape):
    return pl.BlockSpec(shape, lambda i: (0,) * len(shape), pipeline_mode=pl.Buffered(1))


def _mix_out(og, orr, p, x2, wgo, wro, wmo, ln_post, ln_pre, cols, tm):
    t, d = x2.shape
    gla_v, ret_v = og.shape[1], orr.shape[1]
    rg_w = math.gcd(cols["rg"], ret_v)
    n_rg = ret_v // rg_w

    def col_spec(off, w):
        blk = off // w
        return pl.BlockSpec((tm, w), lambda i: (i, blk))

    return pl.pallas_call(
        functools.partial(_mix_kernel, gla_dv=gla_v // GLA_HEADS, ret_dv=ret_v // RET_HEADS,
                          n_rg=n_rg),
        grid=(t // tm,),
        in_specs=[
            pl.BlockSpec((tm, gla_v), lambda i: (i, 0)),
            pl.BlockSpec((tm, ret_v), lambda i: (i, 0)),
            col_spec(cols["gr"], gla_v),
            *[col_spec(cols["rg"] + j * rg_w, rg_w) for j in range(n_rg)],
            col_spec(cols["ag"], d),
            col_spec(cols["ar"], d),
            pl.BlockSpec((tm, d), lambda i: (i, 0)),
            _resident(wgo.shape),
            _resident(wro.shape),
            _resident(wmo.shape),
            _resident((1, d)),
            _resident((1, d)),
        ],
        out_specs=[
            pl.BlockSpec((tm, d), lambda i: (i, 0)),
            pl.BlockSpec((tm, d), lambda i: (i, 0)),
        ],
        out_shape=[
            jax.ShapeDtypeStruct((t, d), F32),
            jax.ShapeDtypeStruct((t, d), BF16),
        ],
        compiler_params=pltpu.CompilerParams(
            dimension_semantics=("parallel",),
            vmem_limit_bytes=VMEM_LIMIT),
        name="mix_out",
    )(og, orr, p, *([p] * n_rg), p, p, x2, wgo, wro, wmo, ln_post, ln_pre)


_GELU_C = math.sqrt(2.0 / math.pi)


def _ffn_kernel(h_ref, halo_ref, x1_ref, wup_ref, cw_ref, cb_ref, wdn_ref, ln_ref, o_ref,
                u0_ref, u1_ref, u2_ref, u3_ref, f_ref, z_ref, *, tiles_per_seq, d_ff, tn):
    u_ref = (u0_ref, u1_ref, u2_ref, u3_ref)
    tm, d = o_ref.shape
    n_phase = CONV_PHASES
    rows_pp = tm // n_phase
    first = (pl.program_id(0) % tiles_per_seq) == 0
    halo = jnp.where(first, jnp.zeros_like(halo_ref[...]), halo_ref[...])
    h_ext = jnp.concatenate([halo, h_ref[...]], axis=0)

    def up(c):
        for half in range(2):
            c0 = half * d_ff + c * tn
            u = _dot(h_ext, wup_ref[:, c0:c0 + tn])
            for l in range(tn // LANES):
                u_ref[2 * (c % 2) + half][l] = u[:, l * LANES:(l + 1) * LANES]

    def conv(slot, c0, gain):
        slabs = []
        for l in range(tn // LANES):
            cl = slice(c0 + l * LANES, c0 + (l + 1) * LANES)
            taps = [cw_ref[j:j + 1, cl] * gain for j in range(CONV_W)]
            bias = cb_ref[:, cl] * gain
            phases = []
            for s in range(n_phase):
                y = bias
                for j in range(CONV_W):
                    r0 = BF16_ROWS - (CONV_W - 1) + j + s
                    y = y + u_ref[slot][l, pl.ds(r0, rows_pp, stride=n_phase), :] * taps[j]
                phases.append(y)
            slabs.append(jnp.concatenate(phases, axis=0))
        return jnp.concatenate(slabs, axis=1)

    n_tiles = d_ff // tn
    up(0)
    for c in range(n_tiles):
        if c + 1 < n_tiles:
            up(c + 1)
        half_val = conv(2 * (c % 2), c * tn, 0.5)
        g = conv(2 * (c % 2) + 1, d_ff + c * tn, 1.0)
        t = jnp.tanh(g * (_GELU_C + (_GELU_C * 0.044715) * (g * g)))
        f_ref[:, c * tn:(c + 1) * tn] = ((g + g * t) * half_val).astype(BF16)

    z = _rms(_dot(f_ref[...], wdn_ref[...])) * ln_ref[...]
    for l in range(d // LANES):
        for s in range(n_phase):
            z_ref[l, pl.ds(s, rows_pp, stride=n_phase), :] = (
                z[s * rows_pp:(s + 1) * rows_pp, l * LANES:(l + 1) * LANES])
    o_ref[...] = x1_ref[...] + jnp.concatenate([z_ref[l] for l in range(d // LANES)], axis=1)


def _conv_ffn(h2, x1, wup, cw, cb, wdn, ln, seq, tm, tn):
    t, d = x1.shape
    d_ff = wdn.shape[0]
    halo_blocks = tm // BF16_ROWS
    return pl.pallas_call(
        functools.partial(_ffn_kernel, tiles_per_seq=seq // tm, d_ff=d_ff, tn=tn),
        grid=(t // tm,),
        in_specs=[
            pl.BlockSpec((tm, d), lambda i: (i, 0)),
            pl.BlockSpec((BF16_ROWS, d), lambda i: (jnp.maximum(i * halo_blocks - 1, 0), 0)),
            pl.BlockSpec((tm, d), lambda i: (i, 0)),
            _resident(wup.shape),
            _resident(cw.shape),
            _resident(cb.shape),
            _resident(wdn.shape),
            _resident((1, d)),
        ],
        out_specs=pl.BlockSpec((tm, d), lambda i: (i, 0)),
        out_shape=jax.ShapeDtypeStruct((t, d), F32),
        scratch_shapes=[pltpu.VMEM((tn // LANES, BF16_ROWS + tm, LANES), F32)] * 4
        + [pltpu.VMEM((tm, d_ff), BF16), pltpu.VMEM((d // LANES, tm, LANES), F32)],
        compiler_params=pltpu.CompilerParams(
            dimension_semantics=("parallel",),
            vmem_limit_bytes=VMEM_LIMIT),
        name="conv_ffn",
    )(h2, h2, x1, wup, cw, cb, wdn, ln)


def _largest_tile(n, cap, quantum):
    best = quantum
    for cand in range(quantum, min(n, cap) + 1, quantum):
        if n % cand == 0:
            best = cand
    return best


def kernel(x, positions, ln_pre_mix, w_in, w_gla_decay, b_gla_decay, w_gla_out, w_ret_out,
           w_mix_out, ln_post_mix, ln_pre_ffn, w_ffn_up, conv_w, conv_b, w_ffn_down, ln_post_ffn):
    b, s, d = x.shape
    depth = w_in.shape[0]
    gla_qk = w_gla_decay.shape[2]
    gla_v = w_gla_out.shape[1]
    ret_v = w_ret_out.shape[1]
    ret_qk = ret_v // 2
    d_ff = w_ffn_down.shape[1]
    ret_dk, ret_dv = ret_qk // RET_HEADS, ret_v // RET_HEADS

    names = ("gq", "gk", "gv", "gr", "glr", "rq", "rk", "rv", "rg", "ag", "ar")
    widths = (gla_qk, gla_qk, gla_v, gla_v, GLA_GATE_RANK, ret_qk, ret_qk, ret_v, ret_v, d, d)
    lr0 = sum(widths[:names.index("glr")])
    cols, off = {}, 0
    for nm, w in zip(names, widths):
        if nm != "glr":
            cols[nm] = off
            off += w
    n_main = off

    inv_freq = (ROPE_BASE ** (-jnp.arange(0, ret_dk, 2, dtype=F32) / ret_dk))[None, :]

    tm_in = _largest_tile(b * s, 1024, 256)
    tn_in = _largest_tile(n_main, 2816, 256)
    tb = _largest_tile(s, 256, CHUNK)
    tm_mix = _largest_tile(s, 512, 256)
    tm_ffn = _largest_tile(s, 512, 256)
    tn_ffn = _largest_tile(d_ff, 256, 256)
    pos0 = positions[:, ::tb]

    x2 = x.reshape(b * s, d)
    for layer in range(depth):
        w = w_in[layer]
        w_main = _drop_columns(w, lr0, GLA_GATE_RANK, _largest_tile(n_main, 1024, 256))
        w_lr = jnp.pad(w.T[lr0:lr0 + GLA_GATE_RANK],
                       ((0, LANES - GLA_GATE_RANK), (0, 0))).astype(BF16)
        wdec = jnp.pad(w_gla_decay[layer], ((0, LANES - GLA_GATE_RANK), (0, 0))).astype(BF16)
        bdec = b_gla_decay[layer][None, :]

        p, glr = _in_proj(x2, ln_pre_mix[layer][None, :], w_main, w_lr, tm_in, tn_in)
        p3 = p.reshape(b, s, -1)
        og, orr = _mixers(p3, glr.reshape(b, s, LANES), wdec, bdec, pos0, inv_freq, cols,
                          ret_dk, ret_dv, tb)
        x1, h2 = _mix_out(og.reshape(b * s, gla_v), orr.reshape(b * s, ret_v), p, x2,
                          w_gla_out[layer].astype(BF16), w_ret_out[layer].astype(BF16),
                          w_mix_out[layer].astype(BF16), ln_post_mix[layer][None, :],
                          ln_pre_ffn[layer][None, :], cols, tm_mix)
        x2 = _conv_ffn(h2, x1, w_ffn_up[layer].astype(BF16), conv_w[layer],
                       conv_b[layer][None, :], w_ffn_down[layer].astype(BF16),
                       ln_post_ffn[layer][None, :], s, tm_ffn, tn_ffn)
    return x2.reshape(b, s, d)
```

```python
import functools
import math

import jax
import jax.numpy as jnp
from jax import lax
from jax.experimental import pallas as pl
from jax.experimental.pallas import tpu as pltpu

F32 = jnp.float32
BF16 = jnp.bfloat16

CHUNK = 64
SUB = 16
EPS = 1e-6
GLA_HEADS = 4
GLA_GATE_RANK = 16
GLA_TAU = 16.0
RET_HEADS = 4
ROPE_BASE = 10000.0
CONV_W = 3
LANES = 128
BF16_ROWS = 16
MIX_ROWS = 256
IN_ROWS = 512
CONV_PHASES = 4
DOWN_GROUPS = 4

VMEM_LIMIT = 56 * 1024 * 1024


def _rms(x):
    return x * lax.rsqrt(jnp.mean(x * x, axis=-1, keepdims=True) + EPS)


def _sigmoid(x):
    return 0.5 + 0.5 * jnp.tanh(0.5 * x)


def _dot(a, b):
    return jnp.dot(a, b, preferred_element_type=F32)


def _dot_nt(a, b):
    return lax.dot_general(a, b, (((1,), (1,)), ((), ())), preferred_element_type=F32)


def _dot_tn(a, b):
    return lax.dot_general(a, b, (((0,), (0,)), ((), ())), preferred_element_type=F32)


def _head_norm_gate(o, gate):
    mu = jnp.mean(o, axis=-1, keepdims=True)
    d = o - mu
    var = jnp.mean(d * d, axis=-1, keepdims=True)
    h = 0.5 * gate
    return (d * lax.rsqrt(var + EPS)).astype(BF16) * (h + h * jnp.tanh(h))


def _in_proj_kernel(x_ref, lnw_ref, w_ref, wlr_ref, p_ref, glr_ref, h_ref):
    j = pl.program_id(1)

    @pl.when(j == 0)
    def _():
        for r in range(x_ref.shape[0] // IN_ROWS):
            rs = slice(r * IN_ROWS, (r + 1) * IN_ROWS)
            h = (_rms(x_ref[rs, :]) * lnw_ref[...]).astype(BF16)
            h_ref[rs, :] = h
            glr_ref[rs, :] = _dot_nt(h, wlr_ref[...]).astype(BF16)
            p_ref[rs, :] = _dot(h, w_ref[...]).astype(BF16)

    @pl.when(j != 0)
    def _():
        p_ref[...] = _dot(h_ref[...], w_ref[...]).astype(BF16)


def _regroup_kernel(wt_ref, o_ref):
    o_ref[...] = wt_ref[...].T.astype(BF16)


def _drop_columns(w, start, gap, tw):
    k, n = w.shape
    n_out = n - gap
    assert start % tw == 0 and n_out % tw == 0 and gap % BF16_ROWS == 0
    first_shifted = start // tw
    return pl.pallas_call(
        _regroup_kernel,
        grid=(n_out // tw,),
        in_specs=[pl.BlockSpec(
            (pl.Element(tw), pl.Element(k)),
            lambda j: (pl.multiple_of(j * tw + jnp.where(j >= first_shifted, gap, 0), BF16_ROWS),
                       0))],
        out_specs=pl.BlockSpec((k, tw), lambda j: (0, j)),
        out_shape=jax.ShapeDtypeStruct((k, n_out), BF16),
        compiler_params=pltpu.CompilerParams(
            dimension_semantics=("parallel",),
            vmem_limit_bytes=VMEM_LIMIT),
        name="w_in_regroup",
    )(w.T)


def _in_proj(x2, lnw, w_main, w_lr, tm, tn):
    t, d = x2.shape
    nc = w_main.shape[1]
    return pl.pallas_call(
        _in_proj_kernel,
        grid=(t // tm, nc // tn),
        in_specs=[
            pl.BlockSpec((tm, d), lambda i, j: (i, 0)),
            pl.BlockSpec((1, d), lambda i, j: (0, 0)),
            pl.BlockSpec((d, tn), lambda i, j: (0, j)),
            pl.BlockSpec((LANES, d), lambda i, j: (0, 0)),
        ],
        out_specs=[
            pl.BlockSpec((tm, tn), lambda i, j: (i, j)),
            pl.BlockSpec((tm, LANES), lambda i, j: (i, 0)),
        ],
        out_shape=[
            jax.ShapeDtypeStruct((t, nc), BF16),
            jax.ShapeDtypeStruct((t, LANES), BF16),
        ],
        scratch_shapes=[pltpu.VMEM((tm, d), BF16)],
        compiler_params=pltpu.CompilerParams(
            dimension_semantics=("parallel", "arbitrary"),
            vmem_limit_bytes=VMEM_LIMIT),
        name="in_proj",
    )(x2, lnw, w_main, w_lr)


def _chunk_cumsum(x, tri):
    hi = x.astype(BF16)
    r1 = x - hi.astype(F32)
    mid = r1.astype(BF16)
    lo = (r1 - mid.astype(F32)).astype(BF16)
    return _dot(tri, hi) + _dot(tri, mid) + _dot(tri, lo)


def _pad_rows(a, before, total):
    parts = []
    if before:
        parts.append(jnp.zeros((before, a.shape[1]), a.dtype))
    parts.append(a)
    after = total - before - a.shape[0]
    if after:
        parts.append(jnp.zeros((after, a.shape[1]), a.dtype))
    return jnp.concatenate(parts, axis=0) if len(parts) > 1 else a


def _log_decay_prefix(glr, wdec_ref, bdec_ref, tri_ref):
    z = _dot(glr, wdec_ref[...]) + bdec_ref[...]
    log_a = (jnp.minimum(z, 0.0) - jnp.log1p(jnp.exp(-jnp.abs(z)))) * (1.0 / GLA_TAU)
    return _chunk_cumsum(log_a, tri_ref[...])


def _gla_init(glr0_ref, wdec_ref, bdec_ref, state_ref, tri_ref, g_ref):
    tb = tri_ref.shape[0]
    state_ref[...] = jnp.zeros_like(state_ref)
    r = lax.broadcasted_iota(jnp.int32, (tb, tb), 0)
    c = lax.broadcasted_iota(jnp.int32, (tb, tb), 1)
    tri_ref[...] = jnp.where((c <= r) & (c // CHUNK == r // CHUNK), 1.0, 0.0).astype(BF16)
    g_ref[0] = _log_decay_prefix(glr0_ref[...], wdec_ref, bdec_ref, tri_ref)


def _gla_stages(t, q_ref, k_ref, v_ref, glr_next_ref, wdec_ref, bdec_ref, o_ref,
                state_ref, tri_ref, g_ref, dk, dv):
    tb = q_ref.shape[0]
    nsub = CHUNK // SUB
    scale = dk ** -0.5
    row = lax.broadcasted_iota(jnp.int32, (CHUNK, CHUNK), 0)
    col = lax.broadcasted_iota(jnp.int32, (CHUNK, CHUNK), 1)
    lower = row >= col
    n_chunks = tb // CHUNK
    units = [(h, c) for c in range(n_chunks) for h in range(GLA_HEADS)]

    def rows(c):
        return slice(c * CHUNK, (c + 1) * CHUNK)

    def lanes(h, w):
        return slice(h * w, (h + 1) * w)

    prep = {}
    for h, c in units:
        g = g_ref[t % 2, rows(c), lanes(h, dk)]
        q = q_ref[rows(c), lanes(h, dk)].astype(F32) * scale
        k = k_ref[rows(c), lanes(h, dk)].astype(F32)
        refs = [g[i * SUB:i * SUB + 1, :] for i in range(nsub)]
        e = jnp.exp(g - jnp.concatenate(
            [jnp.broadcast_to(rf, (SUB, dk)) for rf in refs], axis=0))
        qe = (q * e).astype(BF16)
        ke = (k * e).astype(BF16)
        q_lo, k_lo, q_up, k_up = [], [], [], []
        for i in range(nsub):
            n = (i + 1) * SUB
            f = jnp.exp(refs[i] - g[:n, :])
            kf = k[:n] * f
            k_lo.append(_pad_rows(kf.astype(BF16), 0, CHUNK))
            q_up.append(_pad_rows((q[:n] * f).astype(BF16), 0, CHUNK))
            q_lo.append(_pad_rows(qe[i * SUB:n], i * SUB, CHUNK))
            k_up.append(_pad_rows(ke[i * SUB:n], i * SUB, CHUNK))
        g_last = g[CHUNK - 1:CHUNK, :]
        prep[h, c] = dict(
            q_lo=jnp.concatenate(q_lo, axis=1), k_lo=jnp.concatenate(k_lo, axis=1),
            q_up=jnp.concatenate(q_up, axis=1), k_up=jnp.concatenate(k_up, axis=1),
            q_in=(q * jnp.exp(g)).astype(BF16),
            k_dec=(kf * jnp.exp(g_last - refs[nsub - 1])).astype(BF16),
            decay=jnp.exp(g_last))
    yield

    scores, incr = {}, {}
    for h, c in units:
        p = prep[h, c]
        s_lo = _dot_nt(p["q_lo"], p["k_lo"])
        s_up = _dot_nt(p["q_up"], p["k_up"])
        scores[h, c] = jnp.where(lower, s_lo, s_up).astype(BF16)
        incr[h, c] = _dot_tn(p["k_dec"], v_ref[rows(c), lanes(h, dv)])
    yield

    g_next = _log_decay_prefix(glr_next_ref[...], wdec_ref, bdec_ref, tri_ref)
    yield

    state_in = {}
    for h in range(GLA_HEADS):
        st = state_ref[h]
        for c in range(n_chunks):
            state_in[h, c] = st.astype(BF16)
            col_decay = jnp.transpose(jnp.broadcast_to(prep[h, c]["decay"], (dk, dk)))
            st = st * jnp.concatenate([col_decay] * (dv // dk), axis=1) + incr[h, c]
        state_ref[h] = st
    yield

    for h, c in units:
        o = _dot(jnp.concatenate([prep[h, c]["q_in"], scores[h, c]], axis=1),
                 jnp.concatenate([state_in[h, c], v_ref[rows(c), lanes(h, dv)]], axis=0))
        o_ref[rows(c), lanes(h, dv)] = o.astype(BF16)
    g_ref[(t + 1) % 2] = g_next


def _ret_init(freq_ref, state_ref, decay_ref, qk_decay_ref, rot_ref, dk):
    tile = decay_ref.shape[1]
    state_ref[...] = jnp.zeros_like(state_ref)
    row = lax.broadcasted_iota(jnp.int32, (tile, tile), 0)
    col = lax.broadcasted_iota(jnp.int32, (tile, tile), 1)
    visible = (col // CHUNK) <= (row // CHUNK)
    dist = jnp.abs(row - col).astype(F32)
    idx = lax.broadcasted_iota(jnp.int32, (tile, LANES), 0).astype(F32)
    for h in range(RET_HEADS):
        log_gamma = math.log(1.0 - 2.0 ** (-5.0 - h))
        decay_ref[h] = jnp.where(visible, jnp.exp(log_gamma * dist), 0.0) * (dk ** -0.5)
        qk_decay_ref[h, 0] = jnp.exp(log_gamma * (idx + 1.0))
        qk_decay_ref[h, 1] = jnp.exp(log_gamma * (tile - 1.0 - idx)) * (dk ** -0.5)
    ang = lax.broadcasted_iota(jnp.int32, (tile, dk // 2), 0).astype(F32) * freq_ref[...]
    rot_ref[0] = jnp.cos(ang)
    rot_ref[1] = jnp.sin(ang)


def _ret_stages(pos0, q_ref, k_ref, v_refs, freq_ref, o_ref,
                state_ref, decay_ref, qk_decay_ref, rot_ref, dk, dv):
    tile = q_ref.shape[0]
    half = dk // 2
    heads_per_part = RET_HEADS // len(v_refs)

    ang0 = pos0.astype(F32) * freq_ref[...]
    c0, s0 = jnp.cos(ang0), jnp.sin(ang0)
    cr, sr = rot_ref[0], rot_ref[1]
    cos = c0 * cr - s0 * sr
    sin = s0 * cr + c0 * sr

    def rope(x):
        x1, x2 = x[:, :half], x[:, half:]
        return jnp.concatenate([x1 * cos - x2 * sin, x1 * sin + x2 * cos], axis=1)

    def widen(tbl):
        return jnp.concatenate([tbl] * (dk // LANES), axis=1)

    heads = range(RET_HEADS)
    log_gamma = [math.log(1.0 - 2.0 ** (-5.0 - h)) for h in heads]

    def v_of(h):
        return v_refs[h // heads_per_part][:, (h % heads_per_part) * dv:
                                           (h % heads_per_part + 1) * dv]

    q = [rope(q_ref[:, h * dk:(h + 1) * dk].astype(F32)) for h in heads]
    k = [rope(k_ref[:, h * dk:(h + 1) * dk].astype(F32)) for h in heads]
    qb = [x.astype(BF16) for x in q]
    kb = [x.astype(BF16) for x in k]
    yield
    s = [_dot_nt(qb[h], kb[h]) for h in heads]
    yield
    sb = [(s[h] * decay_ref[h]).astype(BF16) for h in heads]
    q_in = [(q[h] * widen(qk_decay_ref[h, 0])).astype(BF16) for h in heads]
    k_in = [(k[h] * widen(qk_decay_ref[h, 1])).astype(BF16) for h in heads]
    st_b = [state_ref[h].astype(BF16) for h in heads]
    yield
    o = [_dot(sb[h], v_of(h)) + _dot(q_in[h], st_b[h]) for h in heads]
    incr = [_dot_tn(k_in[h], v_of(h)) for h in heads]
    yield
    for h in heads:
        state_ref[h] = math.exp(log_gamma[h] * tile) * state_ref[h] + incr[h]
        o_ref[:, h * dv:(h + 1) * dv] = o[h].astype(BF16)


def _mixers_kernel(pos0_ref, gq_ref, gk_ref, gv_ref, glr0_ref, glr_next_ref, wdec_ref, bdec_ref,
                   rq_ref, rk_ref, *rest, gla_dk, gla_dv, ret_dk, ret_dv, n_parts):
    rv_refs = rest[:n_parts]
    (freq_ref, og_ref, or_ref, gla_state_ref, tri_ref, g_ref,
     ret_state_ref, decay_ref, qk_decay_ref, rot_ref) = rest[n_parts:]
    t = pl.program_id(1)

    @pl.when(t == 0)
    def _():
        _gla_init(glr0_ref, wdec_ref, bdec_ref, gla_state_ref, tri_ref, g_ref)
        _ret_init(freq_ref, ret_state_ref, decay_ref, qk_decay_ref, rot_ref, ret_dk)

    gla = _gla_stages(t, gq_ref, gk_ref, gv_ref, glr_next_ref, wdec_ref, bdec_ref, og_ref,
                      gla_state_ref, tri_ref, g_ref, gla_dk, gla_dv)
    ret = _ret_stages(pos0_ref[pl.program_id(0), t], rq_ref, rk_ref, rv_refs, freq_ref, or_ref,
                      ret_state_ref, decay_ref, qk_decay_ref, rot_ref, ret_dk, ret_dv)
    for which in "rrgrrggrgg":
        next(ret if which == "r" else gla, None)


def _mixers(p3, glr3, wdec, bdec, pos0, inv_freq, cols, ret_dk, ret_dv, tb):
    b, s, _ = p3.shape
    gla_dk = wdec.shape[1] // GLA_HEADS
    gla_dv = 2 * gla_dk
    gqk_w, gv_w = GLA_HEADS * gla_dk, GLA_HEADS * gla_dv
    rqk_w, rv_w = RET_HEADS * ret_dk, RET_HEADS * ret_dv
    part_w = math.gcd(cols["rv"], rv_w)
    n_parts = rv_w // part_w
    assert part_w % ret_dv == 0
    last = s // tb - 1

    def col_spec(off, w):
        blk = off // w
        return pl.BlockSpec((None, tb, w), lambda i, t, pos: (i, t, blk))

    grid_spec = pltpu.PrefetchScalarGridSpec(
        num_scalar_prefetch=1,
        grid=(b, s // tb),
        in_specs=[
            col_spec(cols["gq"], gqk_w),
            col_spec(cols["gk"], gqk_w),
            col_spec(cols["gv"], gv_w),
            pl.BlockSpec((None, tb, LANES), lambda i, t, pos: (i, 0, 0)),
            pl.BlockSpec((None, tb, LANES), lambda i, t, pos: (i, jnp.minimum(t + 1, last), 0)),
            pl.BlockSpec((LANES, gqk_w), lambda i, t, pos: (0, 0)),
            pl.BlockSpec((1, gqk_w), lambda i, t, pos: (0, 0)),
            col_spec(cols["rq"], rqk_w),
            col_spec(cols["rk"], rqk_w),
            *[col_spec(cols["rv"] + j * part_w, part_w) for j in range(n_parts)],
            pl.BlockSpec((1, ret_dk // 2), lambda i, t, pos: (0, 0)),
        ],
        out_specs=[
            pl.BlockSpec((None, tb, gv_w), lambda i, t, pos: (i, t, 0)),
            pl.BlockSpec((None, tb, rv_w), lambda i, t, pos: (i, t, 0)),
        ],
        scratch_shapes=[
            pltpu.VMEM((GLA_HEADS, gla_dk, gla_dv), F32),
            pltpu.VMEM((tb, tb), BF16),
            pltpu.VMEM((2, tb, gqk_w), F32),
            pltpu.VMEM((RET_HEADS, ret_dk, ret_dv), F32),
            pltpu.VMEM((RET_HEADS, tb, tb), F32),
            pltpu.VMEM((RET_HEADS, 2, tb, LANES), F32),
            pltpu.VMEM((2, tb, ret_dk // 2), F32),
        ],
    )
    return pl.pallas_call(
        functools.partial(_mixers_kernel, gla_dk=gla_dk, gla_dv=gla_dv, ret_dk=ret_dk,
                          ret_dv=ret_dv, n_parts=n_parts),
        grid_spec=grid_spec,
        out_shape=[
            jax.ShapeDtypeStruct((b, s, gv_w), BF16),
            jax.ShapeDtypeStruct((b, s, rv_w), BF16),
        ],
        compiler_params=pltpu.CompilerParams(
            dimension_semantics=("parallel", "arbitrary"),
            vmem_limit_bytes=VMEM_LIMIT),
        name="mixers",
    )(pos0, p3, p3, p3, glr3, glr3, wdec, bdec, p3, p3, *([p3] * n_parts), inv_freq)


def _mix_kernel(*refs, gla_dv, ret_dv, n_rg):
    og_ref, or_ref, gr_ref = refs[:3]
    rg_refs = refs[3:3 + n_rg]
    (ag_ref, ar_ref, x_ref, wgo_ref, wro_ref, wmo_ref,
     ln_post_ref, ln_pre_ref, x1_ref, h2_ref) = refs[3 + n_rg:]

    def normed(o_ref, gate_refs, dv, rs):
        heads_per_ref = gate_refs[0].shape[1] // dv
        outs = []
        for h in range(o_ref.shape[1] // dv):
            gate = gate_refs[h // heads_per_ref][rs, (h % heads_per_ref) * dv:
                                                 (h % heads_per_ref + 1) * dv]
            outs.append(_head_norm_gate(o_ref[rs, h * dv:(h + 1) * dv].astype(F32), gate))
        return jnp.concatenate(outs, axis=1)

    tm = x_ref.shape[0]
    for r in range(tm // MIX_ROWS):
        rs = slice(r * MIX_ROWS, (r + 1) * MIX_ROWS)
        y_gla = _dot(normed(og_ref, (gr_ref,), gla_dv, rs), wgo_ref[...])
        y_ret = _dot(normed(or_ref, rg_refs, ret_dv, rs), wro_ref[...])
        merged = (_sigmoid(ag_ref[rs, :]) * y_gla.astype(BF16)
                  + _sigmoid(ar_ref[rs, :]) * y_ret.astype(BF16))
        mo = _dot(merged, wmo_ref[...])
        x1 = x_ref[rs, :] + _rms(mo) * ln_post_ref[...]
        x1_ref[rs, :] = x1
        h2_ref[rs, :] = (_rms(x1) * ln_pre_ref[...]).astype(BF16)


def _resident(shape):
    return pl.BlockSpec(shape, lambda i: (0,) * len(shape), pipeline_mode=pl.Buffered(1))


def _mix_out(og, orr, p, x2, wgo, wro, wmo, ln_post, ln_pre, cols, tm):
    t, d = x2.shape
    gla_v, ret_v = og.shape[1], orr.shape[1]
    rg_w = math.gcd(cols["rg"], ret_v)
    n_rg = ret_v // rg_w

    def col_spec(off, w):
        blk = off // w
        return pl.BlockSpec((tm, w), lambda i: (i, blk))

    return pl.pallas_call(
        functools.partial(_mix_kernel, gla_dv=gla_v // GLA_HEADS, ret_dv=ret_v // RET_HEADS,
                          n_rg=n_rg),
        grid=(t // tm,),
        in_specs=[
            pl.BlockSpec((tm, gla_v), lambda i: (i, 0)),
            pl.BlockSpec((tm, ret_v), lambda i: (i, 0)),
            col_spec(cols["gr"], gla_v),
            *[col_spec(cols["rg"] + j * rg_w, rg_w) for j in range(n_rg)],
            col_spec(cols["ag"], d),
            col_spec(cols["ar"], d),
            pl.BlockSpec((tm, d), lambda i: (i, 0)),
            _resident(wgo.shape),
            _resident(wro.shape),
            _resident(wmo.shape),
            _resident((1, d)),
            _resident((1, d)),
        ],
        out_specs=[
            pl.BlockSpec((tm, d), lambda i: (i, 0)),
            pl.BlockSpec((tm, d), lambda i: (i, 0)),
        ],
        out_shape=[
            jax.ShapeDtypeStruct((t, d), F32),
            jax.ShapeDtypeStruct((t, d), BF16),
        ],
        compiler_params=pltpu.CompilerParams(
            dimension_semantics=("parallel",),
            vmem_limit_bytes=VMEM_LIMIT),
        name="mix_out",
    )(og, orr, p, *([p] * n_rg), p, p, x2, wgo, wro, wmo, ln_post, ln_pre)


_GELU_C = math.sqrt(2.0 / math.pi)


def _ffn_kernel(h_ref, halo_ref, x1_ref, wup_ref, cw_ref, cb_ref, wdn_ref, ln_ref, o_ref,
                u0_ref, u1_ref, u2_ref, u3_ref, f_ref, z_ref, *, tiles_per_seq, d_ff, tn):
    u_ref = (u0_ref, u1_ref, u2_ref, u3_ref)
    tm, d = o_ref.shape
    n_phase = CONV_PHASES
    rows_pp = tm // n_phase
    first = (pl.program_id(0) % tiles_per_seq) == 0
    halo = jnp.where(first, jnp.zeros_like(halo_ref[...]), halo_ref[...])
    h_ext = jnp.concatenate([halo, h_ref[...]], axis=0)

    def up(c):
        for half in range(2):
            c0 = half * d_ff + c * tn
            u = _dot(h_ext, wup_ref[:, c0:c0 + tn])
            for l in range(tn // LANES):
                u_ref[2 * (c % 2) + half][l] = u[:, l * LANES:(l + 1) * LANES]

    def conv(slot, c0, gain):
        slabs = []
        for l in range(tn // LANES):
            cl = slice(c0 + l * LANES, c0 + (l + 1) * LANES)
            taps = [cw_ref[j:j + 1, cl] * gain for j in range(CONV_W)]
            bias = cb_ref[:, cl] * gain
            phases = []
            for s in range(n_phase):
                y = bias
                for j in range(CONV_W):
                    r0 = BF16_ROWS - (CONV_W - 1) + j + s
                    y = y + u_ref[slot][l, pl.ds(r0, rows_pp, stride=n_phase), :] * taps[j]
                phases.append(y)
            slabs.append(jnp.concatenate(phases, axis=0))
        return jnp.concatenate(slabs, axis=1)

    n_tiles = d_ff // tn
    group = -(-n_tiles // DOWN_GROUPS)
    parts, k0 = [], 0
    up(0)
    for c in range(n_tiles):
        if c + 1 < n_tiles:
            up(c + 1)
        half_val = conv(2 * (c % 2), c * tn, 0.5)
        g = conv(2 * (c % 2) + 1, d_ff + c * tn, 1.0)
        t = jnp.tanh(g * (_GELU_C + (_GELU_C * 0.044715) * (g * g)))
        f_ref[:, c * tn:(c + 1) * tn] = ((g + g * t) * half_val).astype(BF16)
        if (c + 1) % group == 0 or c + 1 == n_tiles:
            k1 = (c + 1) * tn
            parts.append(_dot(f_ref[:, k0:k1], wdn_ref[k0:k1, :]))
            k0 = k1

    z = _rms(functools.reduce(lambda a, b: a + b, parts)) * ln_ref[...]
    for l in range(d // LANES):
        for s in range(n_phase):
            z_ref[l, pl.ds(s, rows_pp, stride=n_phase), :] = (
                z[s * rows_pp:(s + 1) * rows_pp, l * LANES:(l + 1) * LANES])
    o_ref[...] = x1_ref[...] + jnp.concatenate([z_ref[l] for l in range(d // LANES)], axis=1)


def _conv_ffn(h2, x1, wup, cw, cb, wdn, ln, seq, tm, tn):
    t, d = x1.shape
    d_ff = wdn.shape[0]
    halo_blocks = tm // BF16_ROWS
    return pl.pallas_call(
        functools.partial(_ffn_kernel, tiles_per_seq=seq // tm, d_ff=d_ff, tn=tn),
        grid=(t // tm,),
        in_specs=[
            pl.BlockSpec((tm, d), lambda i: (i, 0)),
            pl.BlockSpec((BF16_ROWS, d), lambda i: (jnp.maximum(i * halo_blocks - 1, 0), 0)),
            pl.BlockSpec((tm, d), lambda i: (i, 0)),
            _resident(wup.shape),
            _resident(cw.shape),
            _resident(cb.shape),
            _resident(wdn.shape),
            _resident((1, d)),
        ],
        out_specs=pl.BlockSpec((tm, d), lambda i: (i, 0)),
        out_shape=jax.ShapeDtypeStruct((t, d), F32),
        scratch_shapes=[pltpu.VMEM((tn // LANES, BF16_ROWS + tm, LANES), F32)] * 4
        + [pltpu.VMEM((tm, d_ff), BF16), pltpu.VMEM((d // LANES, tm, LANES), F32)],
        compiler_params=pltpu.CompilerParams(
            dimension_semantics=("parallel",),
            vmem_limit_bytes=VMEM_LIMIT),
        name="conv_ffn",
    )(h2, h2, x1, wup, cw, cb, wdn, ln)


def _largest_tile(n, cap, quantum):
    best = quantum
    for cand in range(quantum, min(n, cap) + 1, quantum):
        if n % cand == 0:
            best = cand
    return best


def kernel(x, positions, ln_pre_mix, w_in, w_gla_decay, b_gla_decay, w_gla_out, w_ret_out,
           w_mix_out, ln_post_mix, ln_pre_ffn, w_ffn_up, conv_w, conv_b, w_ffn_down, ln_post_ffn):
    b, s, d = x.shape
    depth = w_in.shape[0]
    gla_qk = w_gla_decay.shape[2]
    gla_v = w_gla_out.shape[1]
    ret_v = w_ret_out.shape[1]
    ret_qk = ret_v // 2
    d_ff = w_ffn_down.shape[1]
    ret_dk, ret_dv = ret_qk // RET_HEADS, ret_v // RET_HEADS

    names = ("gq", "gk", "gv", "gr", "glr", "rq", "rk", "rv", "rg", "ag", "ar")
    widths = (gla_qk, gla_qk, gla_v, gla_v, GLA_GATE_RANK, ret_qk, ret_qk, ret_v, ret_v, d, d)
    lr0 = sum(widths[:names.index("glr")])
    cols, off = {}, 0
    for nm, w in zip(names, widths):
        if nm != "glr":
            cols[nm] = off
            off += w
    n_main = off

    inv_freq = (ROPE_BASE ** (-jnp.arange(0, ret_dk, 2, dtype=F32) / ret_dk))[None, :]

    tm_in = _largest_tile(b * s, 1024, 256)
    tn_in = _largest_tile(n_main, 2816, 256)
    tb = _largest_tile(s, 256, CHUNK)
    tm_mix = _largest_tile(s, 512, 256)
    tm_ffn = _largest_tile(s, 512, 256)
    tn_ffn = _largest_tile(d_ff, 256, 256)
    pos0 = positions[:, ::tb]

    x2 = x.reshape(b * s, d)
    for layer in range(depth):
        w = w_in[layer]
        w_main = _drop_columns(w, lr0, GLA_GATE_RANK, _largest_tile(n_main, 1024, 256))
        w_lr = jnp.pad(w.T[lr0:lr0 + GLA_GATE_RANK],
                       ((0, LANES - GLA_GATE_RANK), (0, 0))).astype(BF16)
        wdec = jnp.pad(w_gla_decay[layer], ((0, LANES - GLA_GATE_RANK), (0, 0))).astype(BF16)
        bdec = b_gla_decay[layer][None, :]

        p, glr = _in_proj(x2, ln_pre_mix[layer][None, :], w_main, w_lr, tm_in, tn_in)
        p3 = p.reshape(b, s, -1)
        og, orr = _mixers(p3, glr.reshape(b, s, LANES), wdec, bdec, pos0, inv_freq, cols,
                          ret_dk, ret_dv, tb)
        x1, h2 = _mix_out(og.reshape(b * s, gla_v), orr.reshape(b * s, ret_v), p, x2,
                          w_gla_out[layer].astype(BF16), w_ret_out[layer].astype(BF16),
                          w_mix_out[layer].astype(BF16), ln_post_mix[layer][None, :],
                          ln_pre_ffn[layer][None, :], cols, tm_mix)
        x2 = _conv_ffn(h2, x1, w_ffn_up[layer].astype(BF16), conv_w[layer],
                       conv_b[layer][None, :], w_ffn_down[layer].astype(BF16),
                       ln_post_ffn[layer][None, :], s, tm_ffn, tn_ffn)
    return x2.reshape(b, s, d)
```

```python
import functools
import math

import jax
import jax.numpy as jnp
from jax import lax
from jax.experimental import pallas as pl
from jax.experimental.pallas import tpu as pltpu

F32 = jnp.float32
BF16 = jnp.bfloat16

CHUNK = 64
SUB = 16
EPS = 1e-6
GLA_HEADS = 4
GLA_GATE_RANK = 16
GLA_TAU = 16.0
RET_HEADS = 4
ROPE_BASE = 10000.0
CONV_W = 3
LANES = 128
BF16_ROWS = 16
MIX_ROWS = 512
IN_ROWS = 512
CONV_PHASES = 4
DOWN_GROUPS = 4

VMEM_LIMIT = 56 * 1024 * 1024


def _rms(x):
    return x * lax.rsqrt(jnp.mean(x * x, axis=-1, keepdims=True) + EPS)


def _sigmoid(x):
    return 0.5 + 0.5 * jnp.tanh(0.5 * x)


def _dot(a, b):
    return jnp.dot(a, b, preferred_element_type=F32)


def _dot_nt(a, b):
    return lax.dot_general(a, b, (((1,), (1,)), ((), ())), preferred_element_type=F32)


def _dot_tn(a, b):
    return lax.dot_general(a, b, (((0,), (0,)), ((), ())), preferred_element_type=F32)


def _head_norm_gate(o, gate):
    mu = jnp.mean(o, axis=-1, keepdims=True)
    d = o - mu
    var = jnp.mean(d * d, axis=-1, keepdims=True)
    h = 0.5 * gate
    return (d * lax.rsqrt(var + EPS)).astype(BF16) * (h + h * jnp.tanh(h))


def _in_proj_kernel(x_ref, lnw_ref, w_ref, wlr_ref, p_ref, glr_ref, h_ref):
    j = pl.program_id(1)

    @pl.when(j == 0)
    def _():
        for r in range(x_ref.shape[0] // IN_ROWS):
            rs = slice(r * IN_ROWS, (r + 1) * IN_ROWS)
            h = (_rms(x_ref[rs, :]) * lnw_ref[...]).astype(BF16)
            h_ref[rs, :] = h
            glr_ref[rs, :] = _dot_nt(h, wlr_ref[...]).astype(BF16)
            p_ref[rs, :] = _dot(h, w_ref[...]).astype(BF16)

    @pl.when(j != 0)
    def _():
        p_ref[...] = _dot(h_ref[...], w_ref[...]).astype(BF16)


def _regroup_kernel(wt_ref, o_ref):
    o_ref[...] = wt_ref[...].T.astype(BF16)


def _drop_columns(w, start, gap, tw):
    k, n = w.shape
    n_out = n - gap
    assert start % tw == 0 and n_out % tw == 0 and gap % BF16_ROWS == 0
    first_shifted = start // tw
    return pl.pallas_call(
        _regroup_kernel,
        grid=(n_out // tw,),
        in_specs=[pl.BlockSpec(
            (pl.Element(tw), pl.Element(k)),
            lambda j: (pl.multiple_of(j * tw + jnp.where(j >= first_shifted, gap, 0), BF16_ROWS),
                       0))],
        out_specs=pl.BlockSpec((k, tw), lambda j: (0, j)),
        out_shape=jax.ShapeDtypeStruct((k, n_out), BF16),
        compiler_params=pltpu.CompilerParams(
            dimension_semantics=("parallel",),
            vmem_limit_bytes=VMEM_LIMIT),
        name="w_in_regroup",
    )(w.T)


def _in_proj(x2, lnw, w_main, w_lr, tm, tn):
    t, d = x2.shape
    nc = w_main.shape[1]
    return pl.pallas_call(
        _in_proj_kernel,
        grid=(t // tm, nc // tn),
        in_specs=[
            pl.BlockSpec((tm, d), lambda i, j: (i, 0)),
            pl.BlockSpec((1, d), lambda i, j: (0, 0)),
            pl.BlockSpec((d, tn), lambda i, j: (0, j)),
            pl.BlockSpec((LANES, d), lambda i, j: (0, 0)),
        ],
        out_specs=[
            pl.BlockSpec((tm, tn), lambda i, j: (i, j)),
            pl.BlockSpec((tm, LANES), lambda i, j: (i, 0)),
        ],
        out_shape=[
            jax.ShapeDtypeStruct((t, nc), BF16),
            jax.ShapeDtypeStruct((t, LANES), BF16),
        ],
        scratch_shapes=[pltpu.VMEM((tm, d), BF16)],
        compiler_params=pltpu.CompilerParams(
            dimension_semantics=("parallel", "arbitrary"),
            vmem_limit_bytes=VMEM_LIMIT),
        name="in_proj",
    )(x2, lnw, w_main, w_lr)


def _chunk_cumsum(x, tri):
    hi = x.astype(BF16)
    r1 = x - hi.astype(F32)
    mid = r1.astype(BF16)
    lo = (r1 - mid.astype(F32)).astype(BF16)
    return _dot(tri, hi) + _dot(tri, mid) + _dot(tri, lo)


def _pad_rows(a, before, total):
    parts = []
    if before:
        parts.append(jnp.zeros((before, a.shape[1]), a.dtype))
    parts.append(a)
    after = total - before - a.shape[0]
    if after:
        parts.append(jnp.zeros((after, a.shape[1]), a.dtype))
    return jnp.concatenate(parts, axis=0) if len(parts) > 1 else a


def _log_decay_prefix(glr, wdec_ref, bdec_ref, tri_ref):
    z = _dot(glr, wdec_ref[...]) + bdec_ref[...]
    log_a = (jnp.minimum(z, 0.0) - jnp.log1p(jnp.exp(-jnp.abs(z)))) * (1.0 / GLA_TAU)
    return _chunk_cumsum(log_a, tri_ref[...])


def _gla_init(glr0_ref, wdec_ref, bdec_ref, state_ref, tri_ref, g_ref):
    tb = tri_ref.shape[0]
    state_ref[...] = jnp.zeros_like(state_ref)
    r = lax.broadcasted_iota(jnp.int32, (tb, tb), 0)
    c = lax.broadcasted_iota(jnp.int32, (tb, tb), 1)
    tri_ref[...] = jnp.where((c <= r) & (c // CHUNK == r // CHUNK), 1.0, 0.0).astype(BF16)
    g_ref[0] = _log_decay_prefix(glr0_ref[...], wdec_ref, bdec_ref, tri_ref)


def _gla_stages(t, q_ref, k_ref, v_ref, glr_next_ref, wdec_ref, bdec_ref, o_ref,
                state_ref, tri_ref, g_ref, dk, dv):
    tb = q_ref.shape[0]
    nsub = CHUNK // SUB
    scale = dk ** -0.5
    row = lax.broadcasted_iota(jnp.int32, (CHUNK, CHUNK), 0)
    col = lax.broadcasted_iota(jnp.int32, (CHUNK, CHUNK), 1)
    lower = row >= col
    n_chunks = tb // CHUNK
    units = [(h, c) for c in range(n_chunks) for h in range(GLA_HEADS)]

    def rows(c):
        return slice(c * CHUNK, (c + 1) * CHUNK)

    def lanes(h, w):
        return slice(h * w, (h + 1) * w)

    prep = {}
    for h, c in units:
        g = g_ref[t % 2, rows(c), lanes(h, dk)]
        q = q_ref[rows(c), lanes(h, dk)].astype(F32) * scale
        k = k_ref[rows(c), lanes(h, dk)].astype(F32)
        refs = [g[i * SUB:i * SUB + 1, :] for i in range(nsub)]
        e = jnp.exp(g - jnp.concatenate(
            [jnp.broadcast_to(rf, (SUB, dk)) for rf in refs], axis=0))
        qe = (q * e).astype(BF16)
        ke = (k * e).astype(BF16)
        q_lo, k_lo, q_up, k_up = [], [], [], []
        for i in range(nsub):
            n = (i + 1) * SUB
            f = jnp.exp(refs[i] - g[:n, :])
            kf = k[:n] * f
            k_lo.append(_pad_rows(kf.astype(BF16), 0, CHUNK))
            q_up.append(_pad_rows((q[:n] * f).astype(BF16), 0, CHUNK))
            q_lo.append(_pad_rows(qe[i * SUB:n], i * SUB, CHUNK))
            k_up.append(_pad_rows(ke[i * SUB:n], i * SUB, CHUNK))
        g_last = g[CHUNK - 1:CHUNK, :]
        prep[h, c] = dict(
            q_lo=jnp.concatenate(q_lo, axis=1), k_lo=jnp.concatenate(k_lo, axis=1),
            q_up=jnp.concatenate(q_up, axis=1), k_up=jnp.concatenate(k_up, axis=1),
            q_in=(q * jnp.exp(g)).astype(BF16),
            k_dec=(kf * jnp.exp(g_last - refs[nsub - 1])).astype(BF16),
            decay=jnp.exp(g_last))
    yield

    scores, incr = {}, {}
    for h, c in units:
        p = prep[h, c]
        s_lo = _dot_nt(p["q_lo"], p["k_lo"])
        s_up = _dot_nt(p["q_up"], p["k_up"])
        scores[h, c] = jnp.where(lower, s_lo, s_up).astype(BF16)
        incr[h, c] = _dot_tn(p["k_dec"], v_ref[rows(c), lanes(h, dv)])
    yield

    g_next = _log_decay_prefix(glr_next_ref[...], wdec_ref, bdec_ref, tri_ref)
    yield

    state_in = {}
    for h in range(GLA_HEADS):
        st = state_ref[h]
        for c in range(n_chunks):
            state_in[h, c] = st.astype(BF16)
            col_decay = jnp.transpose(jnp.broadcast_to(prep[h, c]["decay"], (dk, dk)))
            st = st * jnp.concatenate([col_decay] * (dv // dk), axis=1) + incr[h, c]
        state_ref[h] = st
    yield

    for h, c in units:
        o = _dot(jnp.concatenate([prep[h, c]["q_in"], scores[h, c]], axis=1),
                 jnp.concatenate([state_in[h, c], v_ref[rows(c), lanes(h, dv)]], axis=0))
        o_ref[rows(c), lanes(h, dv)] = o.astype(BF16)
    g_ref[(t + 1) % 2] = g_next


def _ret_init(freq_ref, state_ref, decay_ref, qk_decay_ref, rot_ref, dk):
    tile = decay_ref.shape[1]
    state_ref[...] = jnp.zeros_like(state_ref)
    row = lax.broadcasted_iota(jnp.int32, (tile, tile), 0)
    col = lax.broadcasted_iota(jnp.int32, (tile, tile), 1)
    visible = (col // CHUNK) <= (row // CHUNK)
    dist = jnp.abs(row - col).astype(F32)
    idx = lax.broadcasted_iota(jnp.int32, (tile, LANES), 0).astype(F32)
    for h in range(RET_HEADS):
        log_gamma = math.log(1.0 - 2.0 ** (-5.0 - h))
        decay_ref[h] = jnp.where(visible, jnp.exp(log_gamma * dist), 0.0) * (dk ** -0.5)
        qk_decay_ref[h, 0] = jnp.exp(log_gamma * (idx + 1.0))
        qk_decay_ref[h, 1] = jnp.exp(log_gamma * (tile - 1.0 - idx)) * (dk ** -0.5)
    ang = lax.broadcasted_iota(jnp.int32, (tile, dk // 2), 0).astype(F32) * freq_ref[...]
    rot_ref[0] = jnp.cos(ang)
    rot_ref[1] = jnp.sin(ang)


def _ret_stages(pos0, q_ref, k_ref, v_refs, freq_ref, o_ref,
                state_ref, decay_ref, qk_decay_ref, rot_ref, dk, dv):
    tile = q_ref.shape[0]
    half = dk // 2
    heads_per_part = RET_HEADS // len(v_refs)

    ang0 = pos0.astype(F32) * freq_ref[...]
    c0, s0 = jnp.cos(ang0), jnp.sin(ang0)
    cr, sr = rot_ref[0], rot_ref[1]
    cos = c0 * cr - s0 * sr
    sin = s0 * cr + c0 * sr

    def rope(x):
        x1, x2 = x[:, :half], x[:, half:]
        return jnp.concatenate([x1 * cos - x2 * sin, x1 * sin + x2 * cos], axis=1)

    def widen(tbl):
        return jnp.concatenate([tbl] * (dk // LANES), axis=1)

    heads = range(RET_HEADS)
    log_gamma = [math.log(1.0 - 2.0 ** (-5.0 - h)) for h in heads]

    def v_of(h):
        return v_refs[h // heads_per_part][:, (h % heads_per_part) * dv:
                                           (h % heads_per_part + 1) * dv]

    q = [rope(q_ref[:, h * dk:(h + 1) * dk].astype(F32)) for h in heads]
    k = [rope(k_ref[:, h * dk:(h + 1) * dk].astype(F32)) for h in heads]
    qb = [x.astype(BF16) for x in q]
    kb = [x.astype(BF16) for x in k]
    yield
    s = [_dot_nt(qb[h], kb[h]) for h in heads]
    yield
    sb = [(s[h] * decay_ref[h]).astype(BF16) for h in heads]
    q_in = [(q[h] * widen(qk_decay_ref[h, 0])).astype(BF16) for h in heads]
    k_in = [(k[h] * widen(qk_decay_ref[h, 1])).astype(BF16) for h in heads]
    st_b = [state_ref[h].astype(BF16) for h in heads]
    yield
    o = [_dot(sb[h], v_of(h)) + _dot(q_in[h], st_b[h]) for h in heads]
    incr = [_dot_tn(k_in[h], v_of(h)) for h in heads]
    yield
    for h in heads:
        state_ref[h] = math.exp(log_gamma[h] * tile) * state_ref[h] + incr[h]
        o_ref[:, h * dv:(h + 1) * dv] = o[h].astype(BF16)


def _mixers_kernel(pos0_ref, gq_ref, gk_ref, gv_ref, glr0_ref, glr_next_ref, wdec_ref, bdec_ref,
                   rq_ref, rk_ref, *rest, gla_dk, gla_dv, ret_dk, ret_dv, n_parts):
    rv_refs = rest[:n_parts]
    (freq_ref, og_ref, or_ref, gla_state_ref, tri_ref, g_ref,
     ret_state_ref, decay_ref, qk_decay_ref, rot_ref) = rest[n_parts:]
    t = pl.program_id(1)

    @pl.when(t == 0)
    def _():
        _gla_init(glr0_ref, wdec_ref, bdec_ref, gla_state_ref, tri_ref, g_ref)
        _ret_init(freq_ref, ret_state_ref, decay_ref, qk_decay_ref, rot_ref, ret_dk)

    gla = _gla_stages(t, gq_ref, gk_ref, gv_ref, glr_next_ref, wdec_ref, bdec_ref, og_ref,
                      gla_state_ref, tri_ref, g_ref, gla_dk, gla_dv)
    ret = _ret_stages(pos0_ref[pl.program_id(0), t], rq_ref, rk_ref, rv_refs, freq_ref, or_ref,
                      ret_state_ref, decay_ref, qk_decay_ref, rot_ref, ret_dk, ret_dv)
    for which in "rrgrrggrgg":
        next(ret if which == "r" else gla, None)


def _mixers(p3, glr3, wdec, bdec, pos0, inv_freq, cols, ret_dk, ret_dv, tb):
    b, s, _ = p3.shape
    gla_dk = wdec.shape[1] // GLA_HEADS
    gla_dv = 2 * gla_dk
    gqk_w, gv_w = GLA_HEADS * gla_dk, GLA_HEADS * gla_dv
    rqk_w, rv_w = RET_HEADS * ret_dk, RET_HEADS * ret_dv
    part_w = math.gcd(cols["rv"], rv_w)
    n_parts = rv_w // part_w
    assert part_w % ret_dv == 0
    last = s // tb - 1

    def col_spec(off, w):
        blk = off // w
        return pl.BlockSpec((None, tb, w), lambda i, t, pos: (i, t, blk))

    grid_spec = pltpu.PrefetchScalarGridSpec(
        num_scalar_prefetch=1,
        grid=(b, s // tb),
        in_specs=[
            col_spec(cols["gq"], gqk_w),
            col_spec(cols["gk"], gqk_w),
            col_spec(cols["gv"], gv_w),
            pl.BlockSpec((None, tb, LANES), lambda i, t, pos: (i, 0, 0)),
            pl.BlockSpec((None, tb, LANES), lambda i, t, pos: (i, jnp.minimum(t + 1, last), 0)),
            pl.BlockSpec((LANES, gqk_w), lambda i, t, pos: (0, 0)),
            pl.BlockSpec((1, gqk_w), lambda i, t, pos: (0, 0)),
            col_spec(cols["rq"], rqk_w),
            col_spec(cols["rk"], rqk_w),
            *[col_spec(cols["rv"] + j * part_w, part_w) for j in range(n_parts)],
            pl.BlockSpec((1, ret_dk // 2), lambda i, t, pos: (0, 0)),
        ],
        out_specs=[
            pl.BlockSpec((None, tb, gv_w), lambda i, t, pos: (i, t, 0)),
            pl.BlockSpec((None, tb, rv_w), lambda i, t, pos: (i, t, 0)),
        ],
        scratch_shapes=[
            pltpu.VMEM((GLA_HEADS, gla_dk, gla_dv), F32),
            pltpu.VMEM((tb, tb), BF16),
            pltpu.VMEM((2, tb, gqk_w), F32),
            pltpu.VMEM((RET_HEADS, ret_dk, ret_dv), F32),
            pltpu.VMEM((RET_HEADS, tb, tb), F32),
            pltpu.VMEM((RET_HEADS, 2, tb, LANES), F32),
            pltpu.VMEM((2, tb, ret_dk // 2), F32),
        ],
    )
    return pl.pallas_call(
        functools.partial(_mixers_kernel, gla_dk=gla_dk, gla_dv=gla_dv, ret_dk=ret_dk,
                          ret_dv=ret_dv, n_parts=n_parts),
        grid_spec=grid_spec,
        out_shape=[
            jax.ShapeDtypeStruct((b, s, gv_w), BF16),
            jax.ShapeDtypeStruct((b, s, rv_w), BF16),
        ],
        compiler_params=pltpu.CompilerParams(
            dimension_semantics=("parallel", "arbitrary"),
            vmem_limit_bytes=VMEM_LIMIT),
        name="mixers",
    )(pos0, p3, p3, p3, glr3, glr3, wdec, bdec, p3, p3, *([p3] * n_parts), inv_freq)


def _mix_kernel(*refs, gla_dv, ret_dv, n_rg):
    og_ref, or_ref, gr_ref = refs[:3]
    rg_refs = refs[3:3 + n_rg]
    (ag_ref, ar_ref, x_ref, wgo_ref, wro_ref, wmo_ref,
     ln_post_ref, ln_pre_ref, x1_ref, h2_ref) = refs[3 + n_rg:]

    def project(o_ref, gate_refs, w_ref, dv, rs):
        heads_per_ref = gate_refs[0].shape[1] // dv
        y = None
        for h in range(o_ref.shape[1] // dv):
            gate = gate_refs[h // heads_per_ref][rs, (h % heads_per_ref) * dv:
                                                 (h % heads_per_ref + 1) * dv]
            o_n = _head_norm_gate(o_ref[rs, h * dv:(h + 1) * dv].astype(F32), gate)
            part = _dot(o_n, w_ref[h * dv:(h + 1) * dv, :])
            y = part if y is None else y + part
        return y

    tm = x_ref.shape[0]
    rs = slice(0, tm)
    y_gla = project(og_ref, (gr_ref,), wgo_ref, gla_dv, rs)
    y_ret = project(or_ref, rg_refs, wro_ref, ret_dv, rs)
    merged = (_sigmoid(ag_ref[...]) * y_gla.astype(BF16)
              + _sigmoid(ar_ref[...]) * y_ret.astype(BF16))
    for r in range(tm // MIX_ROWS):
        rs = slice(r * MIX_ROWS, (r + 1) * MIX_ROWS)
        mo = _dot(merged[rs, :], wmo_ref[...])
        x1 = x_ref[rs, :] + _rms(mo) * ln_post_ref[...]
        x1_ref[rs, :] = x1
        h2_ref[rs, :] = (_rms(x1) * ln_pre_ref[...]).astype(BF16)


def _resident(shape):
    return pl.BlockSpec(shape, lambda i: (0,) * len(shape), pipeline_mode=pl.Buffered(1))


def _mix_out(og, orr, p, x2, wgo, wro, wmo, ln_post, ln_pre, cols, tm):
    t, d = x2.shape
    gla_v, ret_v = og.shape[1], orr.shape[1]
    rg_w = math.gcd(cols["rg"], ret_v)
    n_rg = ret_v // rg_w

    def col_spec(off, w):
        blk = off // w
        return pl.BlockSpec((tm, w), lambda i: (i, blk))

    return pl.pallas_call(
        functools.partial(_mix_kernel, gla_dv=gla_v // GLA_HEADS, ret_dv=ret_v // RET_HEADS,
                          n_rg=n_rg),
        grid=(t // tm,),
        in_specs=[
            pl.BlockSpec((tm, gla_v), lambda i: (i, 0)),
            pl.BlockSpec((tm, ret_v), lambda i: (i, 0)),
            col_spec(cols["gr"], gla_v),
            *[col_spec(cols["rg"] + j * rg_w, rg_w) for j in range(n_rg)],
            col_spec(cols["ag"], d),
            col_spec(cols["ar"], d),
            pl.BlockSpec((tm, d), lambda i: (i, 0)),
            _resident(wgo.shape),
            _resident(wro.shape),
            _resident(wmo.shape),
            _resident((1, d)),
            _resident((1, d)),
        ],
        out_specs=[
            pl.BlockSpec((tm, d), lambda i: (i, 0)),
            pl.BlockSpec((tm, d), lambda i: (i, 0)),
        ],
        out_shape=[
            jax.ShapeDtypeStruct((t, d), F32),
            jax.ShapeDtypeStruct((t, d), BF16),
        ],
        compiler_params=pltpu.CompilerParams(
            dimension_semantics=("parallel",),
            vmem_limit_bytes=VMEM_LIMIT),
        name="mix_out",
    )(og, orr, p, *([p] * n_rg), p, p, x2, wgo, wro, wmo, ln_post, ln_pre)


_GELU_C = math.sqrt(2.0 / math.pi)


def _ffn_kernel(h_ref, halo_ref, x1_ref, wup_ref, cw_ref, cb_ref, wdn_ref, ln_ref, o_ref,
                u0_ref, u1_ref, u2_ref, u3_ref, f_ref, z_ref, *, tiles_per_seq, d_ff, tn):
    u_ref = (u0_ref, u1_ref, u2_ref, u3_ref)
    tm, d = o_ref.shape
    n_phase = CONV_PHASES
    rows_pp = tm // n_phase
    first = (pl.program_id(0) % tiles_per_seq) == 0
    halo = jnp.where(first, jnp.zeros_like(halo_ref[...]), halo_ref[...])
    h_ext = jnp.concatenate([halo, h_ref[...]], axis=0)

    def up(c):
        for half in range(2):
            c0 = half * d_ff + c * tn
            u = _dot(h_ext, wup_ref[:, c0:c0 + tn])
            for l in range(tn // LANES):
                u_ref[2 * (c % 2) + half][l] = u[:, l * LANES:(l + 1) * LANES]

    def conv(slot, c0, gain):
        slabs = []
        for l in range(tn // LANES):
            cl = slice(c0 + l * LANES, c0 + (l + 1) * LANES)
            taps = [cw_ref[j:j + 1, cl] * gain for j in range(CONV_W)]
            bias = cb_ref[:, cl] * gain
            phases = []
            for s in range(n_phase):
                y = bias
                for j in range(CONV_W):
                    r0 = BF16_ROWS - (CONV_W - 1) + j + s
                    y = y + u_ref[slot][l, pl.ds(r0, rows_pp, stride=n_phase), :] * taps[j]
                phases.append(y)
            slabs.append(jnp.concatenate(phases, axis=0))
        return jnp.concatenate(slabs, axis=1)

    n_tiles = d_ff // tn
    group = -(-n_tiles // DOWN_GROUPS)
    parts, k0 = [], 0
    up(0)
    for c in range(n_tiles):
        if c + 1 < n_tiles:
            up(c + 1)
        half_val = conv(2 * (c % 2), c * tn, 0.5)
        g = conv(2 * (c % 2) + 1, d_ff + c * tn, 1.0)
        t = jnp.tanh(g * (_GELU_C + (_GELU_C * 0.044715) * (g * g)))
        f_ref[:, c * tn:(c + 1) * tn] = ((g + g * t) * half_val).astype(BF16)
        if (c + 1) % group == 0 or c + 1 == n_tiles:
            k1 = (c + 1) * tn
            parts.append(_dot(f_ref[:, k0:k1], wdn_ref[k0:k1, :]))
            k0 = k1

    z = _rms(functools.reduce(lambda a, b: a + b, parts)) * ln_ref[...]
    for l in range(d // LANES):
        for s in range(n_phase):
            z_ref[l, pl.ds(s, rows_pp, stride=n_phase), :] = (
                z[s * rows_pp:(s + 1) * rows_pp, l * LANES:(l + 1) * LANES])
    o_ref[...] = x1_ref[...] + jnp.concatenate([z_ref[l] for l in range(d // LANES)], axis=1)


def _conv_ffn(h2, x1, wup, cw, cb, wdn, ln, seq, tm, tn):
    t, d = x1.shape
    d_ff = wdn.shape[0]
    halo_blocks = tm // BF16_ROWS
    return pl.pallas_call(
        functools.partial(_ffn_kernel, tiles_per_seq=seq // tm, d_ff=d_ff, tn=tn),
        grid=(t // tm,),
        in_specs=[
            pl.BlockSpec((tm, d), lambda i: (i, 0)),
            pl.BlockSpec((BF16_ROWS, d), lambda i: (jnp.maximum(i * halo_blocks - 1, 0), 0)),
            pl.BlockSpec((tm, d), lambda i: (i, 0)),
            _resident(wup.shape),
            _resident(cw.shape),
            _resident(cb.shape),
            _resident(wdn.shape),
            _resident((1, d)),
        ],
        out_specs=pl.BlockSpec((tm, d), lambda i: (i, 0)),
        out_shape=jax.ShapeDtypeStruct((t, d), F32),
        scratch_shapes=[pltpu.VMEM((tn // LANES, BF16_ROWS + tm, LANES), F32)] * 4
        + [pltpu.VMEM((tm, d_ff), BF16), pltpu.VMEM((d // LANES, tm, LANES), F32)],
        compiler_params=pltpu.CompilerParams(
            dimension_semantics=("parallel",),
            vmem_limit_bytes=VMEM_LIMIT),
        name="conv_ffn",
    )(h2, h2, x1, wup, cw, cb, wdn, ln)


def _largest_tile(n, cap, quantum):
    best = quantum
    for cand in range(quantum, min(n, cap) + 1, quantum):
        if n % cand == 0:
            best = cand
    return best


def kernel(x, positions, ln_pre_mix, w_in, w_gla_decay, b_gla_decay, w_gla_out, w_ret_out,
           w_mix_out, ln_post_mix, ln_pre_ffn, w_ffn_up, conv_w, conv_b, w_ffn_down, ln_post_ffn):
    b, s, d = x.shape
    depth = w_in.shape[0]
    gla_qk = w_gla_decay.shape[2]
    gla_v = w_gla_out.shape[1]
    ret_v = w_ret_out.shape[1]
    ret_qk = ret_v // 2
    d_ff = w_ffn_down.shape[1]
    ret_dk, ret_dv = ret_qk // RET_HEADS, ret_v // RET_HEADS

    names = ("gq", "gk", "gv", "gr", "glr", "rq", "rk", "rv", "rg", "ag", "ar")
    widths = (gla_qk, gla_qk, gla_v, gla_v, GLA_GATE_RANK, ret_qk, ret_qk, ret_v, ret_v, d, d)
    lr0 = sum(widths[:names.index("glr")])
    cols, off = {}, 0
    for nm, w in zip(names, widths):
        if nm != "glr":
            cols[nm] = off
            off += w
    n_main = off

    inv_freq = (ROPE_BASE ** (-jnp.arange(0, ret_dk, 2, dtype=F32) / ret_dk))[None, :]

    tm_in = _largest_tile(b * s, 1024, 256)
    tn_in = _largest_tile(n_main, 2816, 256)
    tb = _largest_tile(s, 256, CHUNK)
    tm_mix = _largest_tile(s, 512, 256)
    tm_ffn = _largest_tile(s, 512, 256)
    tn_ffn = _largest_tile(d_ff, 256, 256)
    pos0 = positions[:, ::tb]

    x2 = x.reshape(b * s, d)
    for layer in range(depth):
        w = w_in[layer]
        w_main = _drop_columns(w, lr0, GLA_GATE_RANK, _largest_tile(n_main, 1024, 256))
        w_lr = jnp.pad(w.T[lr0:lr0 + GLA_GATE_RANK],
                       ((0, LANES - GLA_GATE_RANK), (0, 0))).astype(BF16)
        wdec = jnp.pad(w_gla_decay[layer], ((0, LANES - GLA_GATE_RANK), (0, 0))).astype(BF16)
        bdec = b_gla_decay[layer][None, :]

        p, glr = _in_proj(x2, ln_pre_mix[layer][None, :], w_main, w_lr, tm_in, tn_in)
        p3 = p.reshape(b, s, -1)
        og, orr = _mixers(p3, glr.reshape(b, s, LANES), wdec, bdec, pos0, inv_freq, cols,
                          ret_dk, ret_dv, tb)
        x1, h2 = _mix_out(og.reshape(b * s, gla_v), orr.reshape(b * s, ret_v), p, x2,
                          w_gla_out[layer].astype(BF16), w_ret_out[layer].astype(BF16),
                          w_mix_out[layer].astype(BF16), ln_post_mix[layer][None, :],
                          ln_pre_ffn[layer][None, :], cols, tm_mix)
        x2 = _conv_ffn(h2, x1, w_ffn_up[layer].astype(BF16), conv_w[layer],
                       conv_b[layer][None, :], w_ffn_down[layer].astype(BF16),
                       ln_post_ffn[layer][None, :], s, tm_ffn, tn_ffn)
    return x2.reshape(b, s, d)
```

```python
import functools
import math

import jax
import jax.numpy as jnp
from jax import lax
from jax.experimental import pallas as pl
from jax.experimental.pallas import tpu as pltpu

F32 = jnp.float32
BF16 = jnp.bfloat16

CHUNK = 64
SUB = 16
EPS = 1e-6
GLA_HEADS = 4
GLA_GATE_RANK = 16
GLA_TAU = 16.0
LOG2_E = math.log2(math.e)
RET_HEADS = 4
ROPE_BASE = 10000.0
CONV_W = 3
LANES = 128
BF16_ROWS = 16
MIX_ROWS = 512
IN_ROWS = 512
CONV_PHASES = 4
DOWN_GROUPS = 4

VMEM_LIMIT = 56 * 1024 * 1024


def _rms(x):
    return x * lax.rsqrt(jnp.mean(x * x, axis=-1, keepdims=True) + EPS)


def _sigmoid(x):
    return 0.5 + 0.5 * jnp.tanh(0.5 * x)


def _dot(a, b):
    return jnp.dot(a, b, preferred_element_type=F32)


def _dot_nt(a, b):
    return lax.dot_general(a, b, (((1,), (1,)), ((), ())), preferred_element_type=F32)


def _dot_tn(a, b):
    return lax.dot_general(a, b, (((0,), (0,)), ((), ())), preferred_element_type=F32)


def _head_norm_gate(o, gate):
    mu = jnp.mean(o, axis=-1, keepdims=True)
    d = o - mu
    var = jnp.mean(d * d, axis=-1, keepdims=True)
    h = 0.5 * gate
    return (d * lax.rsqrt(var + EPS)).astype(BF16) * (h + h * jnp.tanh(h))


def _in_proj_kernel(x_ref, lnw_ref, w_ref, wlr_ref, p_ref, glr_ref, h_ref):
    j = pl.program_id(1)

    @pl.when(j == 0)
    def _():
        for r in range(x_ref.shape[0] // IN_ROWS):
            rs = slice(r * IN_ROWS, (r + 1) * IN_ROWS)
            h = (_rms(x_ref[rs, :]) * lnw_ref[...]).astype(BF16)
            h_ref[rs, :] = h
            glr_ref[rs, :] = _dot_nt(h, wlr_ref[...]).astype(BF16)
            p_ref[rs, :] = _dot(h, w_ref[...]).astype(BF16)

    @pl.when(j != 0)
    def _():
        p_ref[...] = _dot(h_ref[...], w_ref[...]).astype(BF16)


def _regroup_kernel(wt_ref, o_ref):
    o_ref[...] = wt_ref[...].T.astype(BF16)


def _drop_columns(w, start, gap, tw):
    k, n = w.shape
    n_out = n - gap
    assert start % tw == 0 and n_out % tw == 0 and gap % BF16_ROWS == 0
    first_shifted = start // tw
    return pl.pallas_call(
        _regroup_kernel,
        grid=(n_out // tw,),
        in_specs=[pl.BlockSpec(
            (pl.Element(tw), pl.Element(k)),
            lambda j: (pl.multiple_of(j * tw + jnp.where(j >= first_shifted, gap, 0), BF16_ROWS),
                       0))],
        out_specs=pl.BlockSpec((k, tw), lambda j: (0, j)),
        out_shape=jax.ShapeDtypeStruct((k, n_out), BF16),
        compiler_params=pltpu.CompilerParams(
            dimension_semantics=("parallel",),
            vmem_limit_bytes=VMEM_LIMIT),
        name="w_in_regroup",
    )(w.T)


def _in_proj(x2, lnw, w_main, w_lr, tm, tn):
    t, d = x2.shape
    nc = w_main.shape[1]
    return pl.pallas_call(
        _in_proj_kernel,
        grid=(t // tm, nc // tn),
        in_specs=[
            pl.BlockSpec((tm, d), lambda i, j: (i, 0)),
            pl.BlockSpec((1, d), lambda i, j: (0, 0)),
            pl.BlockSpec((d, tn), lambda i, j: (0, j)),
            pl.BlockSpec((LANES, d), lambda i, j: (0, 0)),
        ],
        out_specs=[
            pl.BlockSpec((tm, tn), lambda i, j: (i, j)),
            pl.BlockSpec((tm, LANES), lambda i, j: (i, 0)),
        ],
        out_shape=[
            jax.ShapeDtypeStruct((t, nc), BF16),
            jax.ShapeDtypeStruct((t, LANES), BF16),
        ],
        scratch_shapes=[pltpu.VMEM((tm, d), BF16)],
        compiler_params=pltpu.CompilerParams(
            dimension_semantics=("parallel", "arbitrary"),
            vmem_limit_bytes=VMEM_LIMIT),
        name="in_proj",
    )(x2, lnw, w_main, w_lr)


def _chunk_cumsum(x, tri):
    hi = x.astype(BF16)
    r1 = x - hi.astype(F32)
    mid = r1.astype(BF16)
    lo = (r1 - mid.astype(F32)).astype(BF16)
    return _dot(tri, hi) + _dot(tri, mid) + _dot(tri, lo)


def _pad_rows(a, before, total):
    parts = []
    if before:
        parts.append(jnp.zeros((before, a.shape[1]), a.dtype))
    parts.append(a)
    after = total - before - a.shape[0]
    if after:
        parts.append(jnp.zeros((after, a.shape[1]), a.dtype))
    return jnp.concatenate(parts, axis=0) if len(parts) > 1 else a


def _log_decay_prefix(glr, wdec_ref, bdec_ref, tri_ref):
    z = _dot(glr, wdec_ref[...]) + bdec_ref[...]
    log2_a = (jnp.minimum(z, 0.0) * LOG2_E
              - jnp.log2(1.0 + jnp.exp2(jnp.abs(z) * -LOG2_E))) * (1.0 / GLA_TAU)
    return _chunk_cumsum(log2_a, tri_ref[...])


def _gla_init(glr0_ref, wdec_ref, bdec_ref, state_ref, tri_ref, g_ref):
    tb = tri_ref.shape[0]
    state_ref[...] = jnp.zeros_like(state_ref)
    r = lax.broadcasted_iota(jnp.int32, (tb, tb), 0)
    c = lax.broadcasted_iota(jnp.int32, (tb, tb), 1)
    tri_ref[...] = jnp.where((c <= r) & (c // CHUNK == r // CHUNK), 1.0, 0.0).astype(BF16)
    g_ref[0] = _log_decay_prefix(glr0_ref[...], wdec_ref, bdec_ref, tri_ref)


def _gla_stages(t, q_ref, k_ref, v_ref, glr_next_ref, wdec_ref, bdec_ref, o_ref,
                state_ref, tri_ref, g_ref, dk, dv):
    tb = q_ref.shape[0]
    nsub = CHUNK // SUB
    scale = dk ** -0.5
    row = lax.broadcasted_iota(jnp.int32, (CHUNK, CHUNK), 0)
    col = lax.broadcasted_iota(jnp.int32, (CHUNK, CHUNK), 1)
    lower = row >= col
    n_chunks = tb // CHUNK
    units = [(h, c) for c in range(n_chunks) for h in range(GLA_HEADS)]

    def rows(c):
        return slice(c * CHUNK, (c + 1) * CHUNK)

    def lanes(h, w):
        return slice(h * w, (h + 1) * w)

    prep = {}
    for h, c in units:
        g = g_ref[t % 2, rows(c), lanes(h, dk)]
        q = q_ref[rows(c), lanes(h, dk)].astype(F32) * scale
        k = k_ref[rows(c), lanes(h, dk)].astype(F32)
        refs = [g[i * SUB:i * SUB + 1, :] for i in range(nsub)]
        e = jnp.exp2(g - jnp.concatenate(
            [jnp.broadcast_to(rf, (SUB, dk)) for rf in refs], axis=0))
        qe = (q * e).astype(BF16)
        ke = (k * e).astype(BF16)
        q_lo, k_lo, q_up, k_up = [], [], [], []
        for i in range(nsub):
            n = (i + 1) * SUB
            f = jnp.exp2(refs[i] - g[:n, :])
            kf = k[:n] * f
            k_lo.append(_pad_rows(kf.astype(BF16), 0, CHUNK))
            q_up.append(_pad_rows((q[:n] * f).astype(BF16), 0, CHUNK))
            q_lo.append(_pad_rows(qe[i * SUB:n], i * SUB, CHUNK))
            k_up.append(_pad_rows(ke[i * SUB:n], i * SUB, CHUNK))
        g_last = g[CHUNK - 1:CHUNK, :]
        prep[h, c] = dict(
            q_lo=jnp.concatenate(q_lo, axis=1), k_lo=jnp.concatenate(k_lo, axis=1),
            q_up=jnp.concatenate(q_up, axis=1), k_up=jnp.concatenate(k_up, axis=1),
            q_in=(q * jnp.exp2(g)).astype(BF16),
            k_dec=(kf * jnp.exp2(g_last - refs[nsub - 1])).astype(BF16),
            decay=jnp.exp2(g_last))
    yield

    scores, incr = {}, {}
    for h, c in units:
        p = prep[h, c]
        s_lo = _dot_nt(p["q_lo"], p["k_lo"])
        s_up = _dot_nt(p["q_up"], p["k_up"])
        scores[h, c] = jnp.where(lower, s_lo, s_up).astype(BF16)
        incr[h, c] = _dot_tn(p["k_dec"], v_ref[rows(c), lanes(h, dv)])
    yield

    g_ref[(t + 1) % 2] = _log_decay_prefix(glr_next_ref[...], wdec_ref, bdec_ref, tri_ref)
    yield

    state_in = {}
    for h in range(GLA_HEADS):
        st = state_ref[h]
        for c in range(n_chunks):
            state_in[h, c] = st.astype(BF16)
            col_decay = jnp.transpose(jnp.broadcast_to(prep[h, c]["decay"], (dk, dk)))
            st = st * jnp.concatenate([col_decay] * (dv // dk), axis=1) + incr[h, c]
        state_ref[h] = st
    yield

    for h, c in units:
        o = _dot(jnp.concatenate([prep[h, c]["q_in"], scores[h, c]], axis=1),
                 jnp.concatenate([state_in[h, c], v_ref[rows(c), lanes(h, dv)]], axis=0))
        o_ref[rows(c), lanes(h, dv)] = o.astype(BF16)


def _ret_init(freq_ref, state_ref, decay_ref, qk_decay_ref, rot_ref, dk):
    tile = decay_ref.shape[1]
    state_ref[...] = jnp.zeros_like(state_ref)
    row = lax.broadcasted_iota(jnp.int32, (tile, tile), 0)
    col = lax.broadcasted_iota(jnp.int32, (tile, tile), 1)
    visible = (col // CHUNK) <= (row // CHUNK)
    dist = jnp.abs(row - col).astype(F32)
    idx = lax.broadcasted_iota(jnp.int32, (tile, LANES), 0).astype(F32)
    for h in range(RET_HEADS):
        log_gamma = math.log(1.0 - 2.0 ** (-5.0 - h))
        decay_ref[h] = jnp.where(visible, jnp.exp(log_gamma * dist), 0.0) * (dk ** -0.5)
        qk_decay_ref[h, 0] = jnp.exp(log_gamma * (idx + 1.0))
        qk_decay_ref[h, 1] = jnp.exp(log_gamma * (tile - 1.0 - idx)) * (dk ** -0.5)
    ang = lax.broadcasted_iota(jnp.int32, (tile, dk // 2), 0).astype(F32) * freq_ref[...]
    rot_ref[0] = jnp.cos(ang)
    rot_ref[1] = jnp.sin(ang)


def _ret_stages(pos0, q_ref, k_ref, v_refs, freq_ref, o_ref,
                state_ref, decay_ref, qk_decay_ref, rot_ref, dk, dv):
    tile = q_ref.shape[0]
    half = dk // 2
    heads_per_part = RET_HEADS // len(v_refs)

    ang0 = pos0.astype(F32) * freq_ref[...]
    c0, s0 = jnp.cos(ang0), jnp.sin(ang0)
    cr, sr = rot_ref[0], rot_ref[1]
    cos = c0 * cr - s0 * sr
    sin = s0 * cr + c0 * sr

    def rope(x):
        x1, x2 = x[:, :half], x[:, half:]
        return jnp.concatenate([x1 * cos - x2 * sin, x1 * sin + x2 * cos], axis=1)

    def widen(tbl):
        return jnp.concatenate([tbl] * (dk // LANES), axis=1)

    heads = range(RET_HEADS)
    log_gamma = [math.log(1.0 - 2.0 ** (-5.0 - h)) for h in heads]

    def v_of(h):
        return v_refs[h // heads_per_part][:, (h % heads_per_part) * dv:
                                           (h % heads_per_part + 1) * dv]

    q = [rope(q_ref[:, h * dk:(h + 1) * dk].astype(F32)) for h in heads]
    k = [rope(k_ref[:, h * dk:(h + 1) * dk].astype(F32)) for h in heads]
    qb = [x.astype(BF16) for x in q]
    kb = [x.astype(BF16) for x in k]
    yield
    s = [_dot_nt(qb[h], kb[h]) for h in heads]
    yield
    sb = [(s[h] * decay_ref[h]).astype(BF16) for h in heads]
    q_in = [(q[h] * widen(qk_decay_ref[h, 0])).astype(BF16) for h in heads]
    k_in = [(k[h] * widen(qk_decay_ref[h, 1])).astype(BF16) for h in heads]
    st_b = [state_ref[h].astype(BF16) for h in heads]
    yield
    o = [_dot(sb[h], v_of(h)) + _dot(q_in[h], st_b[h]) for h in heads]
    incr = [_dot_tn(k_in[h], v_of(h)) for h in heads]
    yield
    for h in heads:
        state_ref[h] = math.exp(log_gamma[h] * tile) * state_ref[h] + incr[h]
        o_ref[:, h * dv:(h + 1) * dv] = o[h].astype(BF16)


def _mixers_kernel(pos0_ref, gq_ref, gk_ref, gv_ref, glr0_ref, glr_next_ref, wdec_ref, bdec_ref,
                   rq_ref, rk_ref, *rest, gla_dk, gla_dv, ret_dk, ret_dv, n_parts):
    rv_refs = rest[:n_parts]
    (freq_ref, og_ref, or_ref, gla_state_ref, tri_ref, g_ref,
     ret_state_ref, decay_ref, qk_decay_ref, rot_ref) = rest[n_parts:]
    t = pl.program_id(1)

    @pl.when(t == 0)
    def _():
        _gla_init(glr0_ref, wdec_ref, bdec_ref, gla_state_ref, tri_ref, g_ref)
        _ret_init(freq_ref, ret_state_ref, decay_ref, qk_decay_ref, rot_ref, ret_dk)

    gla = _gla_stages(t, gq_ref, gk_ref, gv_ref, glr_next_ref, wdec_ref, bdec_ref, og_ref,
                      gla_state_ref, tri_ref, g_ref, gla_dk, gla_dv)
    ret = _ret_stages(pos0_ref[pl.program_id(0), t], rq_ref, rk_ref, rv_refs, freq_ref, or_ref,
                      ret_state_ref, decay_ref, qk_decay_ref, rot_ref, ret_dk, ret_dv)
    for which in "rrgrrggrgg":
        next(ret if which == "r" else gla, None)


def _mixers(p3, glr3, wdec, bdec, pos0, inv_freq, cols, ret_dk, ret_dv, tb):
    b, s, _ = p3.shape
    gla_dk = wdec.shape[1] // GLA_HEADS
    gla_dv = 2 * gla_dk
    gqk_w, gv_w = GLA_HEADS * gla_dk, GLA_HEADS * gla_dv
    rqk_w, rv_w = RET_HEADS * ret_dk, RET_HEADS * ret_dv
    part_w = math.gcd(cols["rv"], rv_w)
    n_parts = rv_w // part_w
    assert part_w % ret_dv == 0
    last = s // tb - 1

    def col_spec(off, w):
        blk = off // w
        return pl.BlockSpec((None, tb, w), lambda i, t, pos: (i, t, blk))

    grid_spec = pltpu.PrefetchScalarGridSpec(
        num_scalar_prefetch=1,
        grid=(b, s // tb),
        in_specs=[
            col_spec(cols["gq"], gqk_w),
            col_spec(cols["gk"], gqk_w),
            col_spec(cols["gv"], gv_w),
            pl.BlockSpec((None, tb, LANES), lambda i, t, pos: (i, 0, 0)),
            pl.BlockSpec((None, tb, LANES), lambda i, t, pos: (i, jnp.minimum(t + 1, last), 0)),
            pl.BlockSpec((LANES, gqk_w), lambda i, t, pos: (0, 0)),
            pl.BlockSpec((1, gqk_w), lambda i, t, pos: (0, 0)),
            col_spec(cols["rq"], rqk_w),
            col_spec(cols["rk"], rqk_w),
            *[col_spec(cols["rv"] + j * part_w, part_w) for j in range(n_parts)],
            pl.BlockSpec((1, ret_dk // 2), lambda i, t, pos: (0, 0)),
        ],
        out_specs=[
            pl.BlockSpec((None, tb, gv_w), lambda i, t, pos: (i, t, 0)),
            pl.BlockSpec((None, tb, rv_w), lambda i, t, pos: (i, t, 0)),
        ],
        scratch_shapes=[
            pltpu.VMEM((GLA_HEADS, gla_dk, gla_dv), F32),
            pltpu.VMEM((tb, tb), BF16),
            pltpu.VMEM((2, tb, gqk_w), F32),
            pltpu.VMEM((RET_HEADS, ret_dk, ret_dv), F32),
            pltpu.VMEM((RET_HEADS, tb, tb), F32),
            pltpu.VMEM((RET_HEADS, 2, tb, LANES), F32),
            pltpu.VMEM((2, tb, ret_dk // 2), F32),
        ],
    )
    return pl.pallas_call(
        functools.partial(_mixers_kernel, gla_dk=gla_dk, gla_dv=gla_dv, ret_dk=ret_dk,
                          ret_dv=ret_dv, n_parts=n_parts),
        grid_spec=grid_spec,
        out_shape=[
            jax.ShapeDtypeStruct((b, s, gv_w), BF16),
            jax.ShapeDtypeStruct((b, s, rv_w), BF16),
        ],
        compiler_params=pltpu.CompilerParams(
            dimension_semantics=("parallel", "arbitrary"),
            vmem_limit_bytes=VMEM_LIMIT),
        name="mixers",
    )(pos0, p3, p3, p3, glr3, glr3, wdec, bdec, p3, p3, *([p3] * n_parts), inv_freq)


def _mix_kernel(*refs, gla_dv, ret_dv, n_rg):
    og_ref, or_ref, gr_ref = refs[:3]
    rg_refs = refs[3:3 + n_rg]
    (ag_ref, ar_ref, x_ref, wgo_ref, wro_ref, wmo_ref,
     ln_post_ref, ln_pre_ref, x1_ref, h2_ref) = refs[3 + n_rg:]

    def project(o_ref, gate_refs, w_ref, dv, rs):
        heads_per_ref = gate_refs[0].shape[1] // dv
        y = None
        for h in range(o_ref.shape[1] // dv):
            gate = gate_refs[h // heads_per_ref][rs, (h % heads_per_ref) * dv:
                                                 (h % heads_per_ref + 1) * dv]
            o_n = _head_norm_gate(o_ref[rs, h * dv:(h + 1) * dv].astype(F32), gate)
            part = _dot(o_n, w_ref[h * dv:(h + 1) * dv, :])
            y = part if y is None else y + part
        return y

    tm = x_ref.shape[0]
    rs = slice(0, tm)
    y_gla = project(og_ref, (gr_ref,), wgo_ref, gla_dv, rs)
    y_ret = project(or_ref, rg_refs, wro_ref, ret_dv, rs)
    merged = (_sigmoid(ag_ref[...]) * y_gla.astype(BF16)
              + _sigmoid(ar_ref[...]) * y_ret.astype(BF16))
    for r in range(tm // MIX_ROWS):
        rs = slice(r * MIX_ROWS, (r + 1) * MIX_ROWS)
        mo = _dot(merged[rs, :], wmo_ref[...])
        x1 = x_ref[rs, :] + _rms(mo) * ln_post_ref[...]
        x1_ref[rs, :] = x1
        h2_ref[rs, :] = (_rms(x1) * ln_pre_ref[...]).astype(BF16)


def _resident(shape):
    return pl.BlockSpec(shape, lambda i: (0,) * len(shape), pipeline_mode=pl.Buffered(1))


def _mix_out(og, orr, p, x2, wgo, wro, wmo, ln_post, ln_pre, cols, tm):
    t, d = x2.shape
    gla_v, ret_v = og.shape[1], orr.shape[1]
    rg_w = math.gcd(cols["rg"], ret_v)
    n_rg = ret_v // rg_w

    def col_spec(off, w):
        blk = off // w
        return pl.BlockSpec((tm, w), lambda i: (i, blk))

    return pl.pallas_call(
        functools.partial(_mix_kernel, gla_dv=gla_v // GLA_HEADS, ret_dv=ret_v // RET_HEADS,
                          n_rg=n_rg),
        grid=(t // tm,),
        in_specs=[
            pl.BlockSpec((tm, gla_v), lambda i: (i, 0)),
            pl.BlockSpec((tm, ret_v), lambda i: (i, 0)),
            col_spec(cols["gr"], gla_v),
            *[col_spec(cols["rg"] + j * rg_w, rg_w) for j in range(n_rg)],
            col_spec(cols["ag"], d),
            col_spec(cols["ar"], d),
            pl.BlockSpec((tm, d), lambda i: (i, 0)),
            _resident(wgo.shape),
            _resident(wro.shape),
            _resident(wmo.shape),
            _resident((1, d)),
            _resident((1, d)),
        ],
        out_specs=[
            pl.BlockSpec((tm, d), lambda i: (i, 0)),
            pl.BlockSpec((tm, d), lambda i: (i, 0)),
        ],
        out_shape=[
            jax.ShapeDtypeStruct((t, d), F32),
            jax.ShapeDtypeStruct((t, d), BF16),
        ],
        compiler_params=pltpu.CompilerParams(
            dimension_semantics=("parallel",),
            vmem_limit_bytes=VMEM_LIMIT),
        name="mix_out",
    )(og, orr, p, *([p] * n_rg), p, p, x2, wgo, wro, wmo, ln_post, ln_pre)


_GELU_C = math.sqrt(2.0 / math.pi)


def _ffn_kernel(h_ref, halo_ref, x1_ref, wup_ref, cw_ref, cb_ref, wdn_ref, ln_ref, o_ref,
                u0_ref, u1_ref, u2_ref, u3_ref, f_ref, z_ref, *, tiles_per_seq, d_ff, tn):
    u_ref = (u0_ref, u1_ref, u2_ref, u3_ref)
    tm, d = o_ref.shape
    n_phase = CONV_PHASES
    rows_pp = tm // n_phase
    first = (pl.program_id(0) % tiles_per_seq) == 0
    halo = jnp.where(first, jnp.zeros_like(halo_ref[...]), halo_ref[...])
    h_ext = jnp.concatenate([halo, h_ref[...]], axis=0)

    def up(c):
        for half in range(2):
            c0 = half * d_ff + c * tn
            u = _dot(h_ext, wup_ref[:, c0:c0 + tn])
            for l in range(tn // LANES):
                u_ref[2 * (c % 2) + half][l] = u[:, l * LANES:(l + 1) * LANES]

    def conv(slot, c0, gain):
        slabs = []
        for l in range(tn // LANES):
            cl = slice(c0 + l * LANES, c0 + (l + 1) * LANES)
            taps = [cw_ref[j:j + 1, cl] * gain for j in range(CONV_W)]
            bias = cb_ref[:, cl] * gain
            phases = []
            for s in range(n_phase):
                y = bias
                for j in range(CONV_W):
                    r0 = BF16_ROWS - (CONV_W - 1) + j + s
                    y = y + u_ref[slot][l, pl.ds(r0, rows_pp, stride=n_phase), :] * taps[j]
                phases.append(y)
            slabs.append(jnp.concatenate(phases, axis=0))
        return jnp.concatenate(slabs, axis=1)

    n_tiles = d_ff // tn
    group = -(-n_tiles // DOWN_GROUPS)
    parts, k0 = [], 0
    up(0)
    for c in range(n_tiles):
        if c + 1 < n_tiles:
            up(c + 1)
        half_val = conv(2 * (c % 2), c * tn, 0.5)
        g = conv(2 * (c % 2) + 1, d_ff + c * tn, 1.0)
        t = jnp.tanh(g * (_GELU_C + (_GELU_C * 0.044715) * (g * g)))
        f_ref[:, c * tn:(c + 1) * tn] = ((g + g * t) * half_val).astype(BF16)
        if (c + 1) % group == 0 or c + 1 == n_tiles:
            k1 = (c + 1) * tn
            parts.append(_dot(f_ref[:, k0:k1], wdn_ref[k0:k1, :]))
            k0 = k1

    z = _rms(functools.reduce(lambda a, b: a + b, parts)) * ln_ref[...]
    for l in range(d // LANES):
        for s in range(n_phase):
            z_ref[l, pl.ds(s, rows_pp, stride=n_phase), :] = (
                z[s * rows_pp:(s + 1) * rows_pp, l * LANES:(l + 1) * LANES])
    o_ref[...] = x1_ref[...] + jnp.concatenate([z_ref[l] for l in range(d // LANES)], axis=1)


def _conv_ffn(h2, x1, wup, cw, cb, wdn, ln, seq, tm, tn):
    t, d = x1.shape
    d_ff = wdn.shape[0]
    halo_blocks = tm // BF16_ROWS
    return pl.pallas_call(
        functools.partial(_ffn_kernel, tiles_per_seq=seq // tm, d_ff=d_ff, tn=tn),
        grid=(t // tm,),
        in_specs=[
            pl.BlockSpec((tm, d), lambda i: (i, 0)),
            pl.BlockSpec((BF16_ROWS, d), lambda i: (jnp.maximum(i * halo_blocks - 1, 0), 0)),
            pl.BlockSpec((tm, d), lambda i: (i, 0)),
            _resident(wup.shape),
            _resident(cw.shape),
            _resident(cb.shape),
            _resident(wdn.shape),
            _resident((1, d)),
        ],
        out_specs=pl.BlockSpec((tm, d), lambda i: (i, 0)),
        out_shape=jax.ShapeDtypeStruct((t, d), F32),
        scratch_shapes=[pltpu.VMEM((tn // LANES, BF16_ROWS + tm, LANES), F32)] * 4
        + [pltpu.VMEM((tm, d_ff), BF16), pltpu.VMEM((d // LANES, tm, LANES), F32)],
        compiler_params=pltpu.CompilerParams(
            dimension_semantics=("parallel",),
            vmem_limit_bytes=VMEM_LIMIT),
        name="conv_ffn",
    )(h2, h2, x1, wup, cw, cb, wdn, ln)


def _largest_tile(n, cap, quantum):
    best = quantum
    for cand in range(quantum, min(n, cap) + 1, quantum):
        if n % cand == 0:
            best = cand
    return best


def kernel(x, positions, ln_pre_mix, w_in, w_gla_decay, b_gla_decay, w_gla_out, w_ret_out,
           w_mix_out, ln_post_mix, ln_pre_ffn, w_ffn_up, conv_w, conv_b, w_ffn_down, ln_post_ffn):
    b, s, d = x.shape
    depth = w_in.shape[0]
    gla_qk = w_gla_decay.shape[2]
    gla_v = w_gla_out.shape[1]
    ret_v = w_ret_out.shape[1]
    ret_qk = ret_v // 2
    d_ff = w_ffn_down.shape[1]
    ret_dk, ret_dv = ret_qk // RET_HEADS, ret_v // RET_HEADS

    names = ("gq", "gk", "gv", "gr", "glr", "rq", "rk", "rv", "rg", "ag", "ar")
    widths = (gla_qk, gla_qk, gla_v, gla_v, GLA_GATE_RANK, ret_qk, ret_qk, ret_v, ret_v, d, d)
    lr0 = sum(widths[:names.index("glr")])
    cols, off = {}, 0
    for nm, w in zip(names, widths):
        if nm != "glr":
            cols[nm] = off
            off += w
    n_main = off

    inv_freq = (ROPE_BASE ** (-jnp.arange(0, ret_dk, 2, dtype=F32) / ret_dk))[None, :]

    tm_in = _largest_tile(b * s, 1024, 256)
    tn_in = _largest_tile(n_main, 2816, 256)
    tb = _largest_tile(s, 256, CHUNK)
    tm_mix = _largest_tile(s, 512, 256)
    tm_ffn = _largest_tile(s, 512, 256)
    tn_ffn = _largest_tile(d_ff, 256, 256)
    pos0 = positions[:, ::tb]

    x2 = x.reshape(b * s, d)
    for layer in range(depth):
        w = w_in[layer]
        w_main = _drop_columns(w, lr0, GLA_GATE_RANK, _largest_tile(n_main, 1024, 256))
        w_lr = jnp.pad(w.T[lr0:lr0 + GLA_GATE_RANK],
                       ((0, LANES - GLA_GATE_RANK), (0, 0))).astype(BF16)
        wdec = jnp.pad(w_gla_decay[layer], ((0, LANES - GLA_GATE_RANK), (0, 0))).astype(BF16)
        bdec = b_gla_decay[layer][None, :]

        p, glr = _in_proj(x2, ln_pre_mix[layer][None, :], w_main, w_lr, tm_in, tn_in)
        p3 = p.reshape(b, s, -1)
        og, orr = _mixers(p3, glr.reshape(b, s, LANES), wdec, bdec, pos0, inv_freq, cols,
                          ret_dk, ret_dv, tb)
        x1, h2 = _mix_out(og.reshape(b * s, gla_v), orr.reshape(b * s, ret_v), p, x2,
                          w_gla_out[layer].astype(BF16), w_ret_out[layer].astype(BF16),
                          w_mix_out[layer].astype(BF16), ln_post_mix[layer][None, :],
                          ln_pre_ffn[layer][None, :], cols, tm_mix)
        x2 = _conv_ffn(h2, x1, w_ffn_up[layer].astype(BF16), conv_w[layer],
                       conv_b[layer][None, :], w_ffn_down[layer].astype(BF16),
                       ln_post_ffn[layer][None, :], s, tm_ffn, tn_ffn)
    return x2.reshape(b, s, d)
```

```python
import functools
import math

import jax
import jax.numpy as jnp
from jax import lax
from jax.experimental import pallas as pl
from jax.experimental.pallas import tpu as pltpu

F32 = jnp.float32
BF16 = jnp.bfloat16

CHUNK = 64
SUB = 16
EPS = 1e-6
GLA_HEADS = 4
GLA_GATE_RANK = 16
GLA_TAU = 16.0
LOG2_E = math.log2(math.e)
RET_HEADS = 4
ROPE_BASE = 10000.0
CONV_W = 3
LANES = 128
BF16_ROWS = 16
MIX_ROWS = 512
IN_ROWS = 512
CONV_PHASES = 4
DOWN_GROUPS = 4

VMEM_LIMIT = 56 * 1024 * 1024


def _rms(x):
    return x * lax.rsqrt(jnp.mean(x * x, axis=-1, keepdims=True) + EPS)


def _sigmoid(x):
    return 0.5 + 0.5 * jnp.tanh(0.5 * x)


def _dot(a, b):
    return jnp.dot(a, b, preferred_element_type=F32)


def _dot_nt(a, b):
    return lax.dot_general(a, b, (((1,), (1,)), ((), ())), preferred_element_type=F32)


def _dot_tn(a, b):
    return lax.dot_general(a, b, (((0,), (0,)), ((), ())), preferred_element_type=F32)


def _head_norm_gate(o, gate):
    mu = jnp.mean(o, axis=-1, keepdims=True)
    d = o - mu
    var = jnp.mean(d * d, axis=-1, keepdims=True)
    h = 0.5 * gate
    return (d * lax.rsqrt(var + EPS)).astype(BF16) * (h + h * jnp.tanh(h))


def _in_proj_kernel(x_ref, lnw_ref, w_ref, wlr_ref, p_ref, glr_ref, h_ref):
    j = pl.program_id(1)

    @pl.when(j == 0)
    def _():
        rows = math.gcd(IN_ROWS, x_ref.shape[0])
        for r in range(x_ref.shape[0] // rows):
            rs = slice(r * rows, (r + 1) * rows)
            h = (_rms(x_ref[rs, :]) * lnw_ref[...]).astype(BF16)
            h_ref[rs, :] = h
            glr_ref[rs, :] = _dot_nt(h, wlr_ref[...]).astype(BF16)
            p_ref[rs, :] = _dot(h, w_ref[...]).astype(BF16)

    @pl.when(j != 0)
    def _():
        p_ref[...] = _dot(h_ref[...], w_ref[...]).astype(BF16)


def _regroup_kernel(wt_ref, o_ref):
    o_ref[...] = wt_ref[...].T.astype(BF16)


def _drop_columns(w, start, gap, tw):
    k, n = w.shape
    n_out = n - gap
    assert start % tw == 0 and n_out % tw == 0 and gap % BF16_ROWS == 0
    first_shifted = start // tw
    return pl.pallas_call(
        _regroup_kernel,
        grid=(n_out // tw,),
        in_specs=[pl.BlockSpec(
            (pl.Element(tw), pl.Element(k)),
            lambda j: (pl.multiple_of(j * tw + jnp.where(j >= first_shifted, gap, 0), BF16_ROWS),
                       0))],
        out_specs=pl.BlockSpec((k, tw), lambda j: (0, j)),
        out_shape=jax.ShapeDtypeStruct((k, n_out), BF16),
        compiler_params=pltpu.CompilerParams(
            dimension_semantics=("parallel",),
            vmem_limit_bytes=VMEM_LIMIT),
        name="w_in_regroup",
    )(w.T)


def _in_proj(x2, lnw, w_main, w_lr, tm, tn):
    t, d = x2.shape
    nc = w_main.shape[1]
    return pl.pallas_call(
        _in_proj_kernel,
        grid=(t // tm, nc // tn),
        in_specs=[
            pl.BlockSpec((tm, d), lambda i, j: (i, 0)),
            pl.BlockSpec((1, d), lambda i, j: (0, 0)),
            pl.BlockSpec((d, tn), lambda i, j: (0, j)),
            pl.BlockSpec((LANES, d), lambda i, j: (0, 0)),
        ],
        out_specs=[
            pl.BlockSpec((tm, tn), lambda i, j: (i, j)),
            pl.BlockSpec((tm, LANES), lambda i, j: (i, 0)),
        ],
        out_shape=[
            jax.ShapeDtypeStruct((t, nc), BF16),
            jax.ShapeDtypeStruct((t, LANES), BF16),
        ],
        scratch_shapes=[pltpu.VMEM((tm, d), BF16)],
        compiler_params=pltpu.CompilerParams(
            dimension_semantics=("parallel", "arbitrary"),
            vmem_limit_bytes=VMEM_LIMIT),
        name="in_proj",
    )(x2, lnw, w_main, w_lr)


def _chunk_cumsum(x, tri):
    hi = x.astype(BF16)
    r1 = x - hi.astype(F32)
    mid = r1.astype(BF16)
    lo = (r1 - mid.astype(F32)).astype(BF16)
    return _dot(tri, hi) + _dot(tri, mid) + _dot(tri, lo)


def _pad_rows(a, before, total):
    parts = []
    if before:
        parts.append(jnp.zeros((before, a.shape[1]), a.dtype))
    parts.append(a)
    after = total - before - a.shape[0]
    if after:
        parts.append(jnp.zeros((after, a.shape[1]), a.dtype))
    return jnp.concatenate(parts, axis=0) if len(parts) > 1 else a


def _log_decay_prefix(glr, wdec_ref, bdec_ref, tri_ref):
    z = _dot(glr, wdec_ref[...]) + bdec_ref[...]
    log2_a = (jnp.minimum(z, 0.0) * LOG2_E
              - jnp.log2(1.0 + jnp.exp2(jnp.abs(z) * -LOG2_E))) * (1.0 / GLA_TAU)
    return _chunk_cumsum(log2_a, tri_ref[...])


def _gla_init(glr0_ref, wdec_ref, bdec_ref, state_ref, tri_ref, g_ref):
    tb = tri_ref.shape[0]
    state_ref[...] = jnp.zeros_like(state_ref)
    r = lax.broadcasted_iota(jnp.int32, (tb, tb), 0)
    c = lax.broadcasted_iota(jnp.int32, (tb, tb), 1)
    tri_ref[...] = jnp.where((c <= r) & (c // CHUNK == r // CHUNK), 1.0, 0.0).astype(BF16)
    g_ref[0] = _log_decay_prefix(glr0_ref[...], wdec_ref, bdec_ref, tri_ref)


def _gla_stages(slot, next_slot, q_ref, k_ref, v_ref, glr_next_ref, wdec_ref, bdec_ref, o_ref,
                state_ref, tri_ref, g_ref, dk, dv):
    tb = q_ref.shape[0]
    nsub = CHUNK // SUB
    scale = dk ** -0.5
    row = lax.broadcasted_iota(jnp.int32, (CHUNK, CHUNK), 0)
    col = lax.broadcasted_iota(jnp.int32, (CHUNK, CHUNK), 1)
    lower = row >= col
    n_chunks = tb // CHUNK
    units = [(h, c) for c in range(n_chunks) for h in range(GLA_HEADS)]

    def rows(c):
        return slice(c * CHUNK, (c + 1) * CHUNK)

    def lanes(h, w):
        return slice(h * w, (h + 1) * w)

    prep = {}
    for h, c in units:
        g = g_ref[slot, rows(c), lanes(h, dk)]
        q = q_ref[rows(c), lanes(h, dk)].astype(F32) * scale
        k = k_ref[rows(c), lanes(h, dk)].astype(F32)
        refs = [g[i * SUB:i * SUB + 1, :] for i in range(nsub)]
        e = jnp.exp2(g - jnp.concatenate(
            [jnp.broadcast_to(rf, (SUB, dk)) for rf in refs], axis=0))
        qe = (q * e).astype(BF16)
        ke = (k * e).astype(BF16)
        q_lo, k_lo, q_up, k_up = [], [], [], []
        for i in range(nsub):
            n = (i + 1) * SUB
            f = jnp.exp2(refs[i] - g[:n, :])
            kf = k[:n] * f
            k_lo.append(_pad_rows(kf.astype(BF16), 0, CHUNK))
            q_up.append(_pad_rows((q[:n] * f).astype(BF16), 0, CHUNK))
            q_lo.append(_pad_rows(qe[i * SUB:n], i * SUB, CHUNK))
            k_up.append(_pad_rows(ke[i * SUB:n], i * SUB, CHUNK))
        g_last = g[CHUNK - 1:CHUNK, :]
        prep[h, c] = dict(
            q_lo=jnp.concatenate(q_lo, axis=1), k_lo=jnp.concatenate(k_lo, axis=1),
            q_up=jnp.concatenate(q_up, axis=1), k_up=jnp.concatenate(k_up, axis=1),
            q_in=(q * jnp.exp2(g)).astype(BF16),
            k_dec=(kf * jnp.exp2(g_last - refs[nsub - 1])).astype(BF16),
            decay=jnp.exp2(g_last))
    yield

    scores, incr = {}, {}
    for h, c in units:
        p = prep[h, c]
        s_lo = _dot_nt(p["q_lo"], p["k_lo"])
        s_up = _dot_nt(p["q_up"], p["k_up"])
        scores[h, c] = jnp.where(lower, s_lo, s_up).astype(BF16)
        incr[h, c] = _dot_tn(p["k_dec"], v_ref[rows(c), lanes(h, dv)])
    yield

    g_ref[next_slot] = _log_decay_prefix(glr_next_ref[...], wdec_ref, bdec_ref, tri_ref)
    yield

    state_in = {}
    for h in range(GLA_HEADS):
        st = state_ref[h]
        for c in range(n_chunks):
            state_in[h, c] = st.astype(BF16)
            col_decay = jnp.transpose(jnp.broadcast_to(prep[h, c]["decay"], (dk, dk)))
            st = st * jnp.concatenate([col_decay] * (dv // dk), axis=1) + incr[h, c]
        state_ref[h] = st
    yield

    for h, c in units:
        o = _dot(jnp.concatenate([prep[h, c]["q_in"], scores[h, c]], axis=1),
                 jnp.concatenate([state_in[h, c], v_ref[rows(c), lanes(h, dv)]], axis=0))
        o_ref[rows(c), lanes(h, dv)] = o.astype(BF16)


def _ret_init(freq_ref, state_ref, decay_ref, qk_decay_ref, rot_ref, dk):
    tile = decay_ref.shape[1]
    state_ref[...] = jnp.zeros_like(state_ref)
    row = lax.broadcasted_iota(jnp.int32, (tile, tile), 0)
    col = lax.broadcasted_iota(jnp.int32, (tile, tile), 1)
    visible = (col // CHUNK) <= (row // CHUNK)
    dist = jnp.abs(row - col).astype(F32)
    idx = lax.broadcasted_iota(jnp.int32, (tile, LANES), 0).astype(F32)
    for h in range(RET_HEADS):
        log_gamma = math.log(1.0 - 2.0 ** (-5.0 - h))
        decay_ref[h] = jnp.where(visible, jnp.exp(log_gamma * dist), 0.0) * (dk ** -0.5)
        qk_decay_ref[h, 0] = jnp.exp(log_gamma * (idx + 1.0))
        qk_decay_ref[h, 1] = jnp.exp(log_gamma * (tile - 1.0 - idx)) * (dk ** -0.5)
    ang = lax.broadcasted_iota(jnp.int32, (tile, dk // 2), 0).astype(F32) * freq_ref[...]
    rot_ref[0] = jnp.cos(ang)
    rot_ref[1] = jnp.sin(ang)


def _ret_stages(pos0, q_ref, k_ref, v_refs, freq_ref, o_ref,
                state_ref, decay_ref, qk_decay_ref, rot_ref, dk, dv):
    tile = q_ref.shape[0]
    half = dk // 2
    heads_per_part = RET_HEADS // len(v_refs)

    ang0 = pos0.astype(F32) * freq_ref[...]
    c0, s0 = jnp.cos(ang0), jnp.sin(ang0)
    cr, sr = rot_ref[0], rot_ref[1]
    cos = c0 * cr - s0 * sr
    sin = s0 * cr + c0 * sr

    def rope(x):
        x1, x2 = x[:, :half], x[:, half:]
        return jnp.concatenate([x1 * cos - x2 * sin, x1 * sin + x2 * cos], axis=1)

    def widen(tbl):
        return jnp.concatenate([tbl] * (dk // LANES), axis=1)

    heads = range(RET_HEADS)
    log_gamma = [math.log(1.0 - 2.0 ** (-5.0 - h)) for h in heads]

    def v_of(h):
        return v_refs[h // heads_per_part][:, (h % heads_per_part) * dv:
                                           (h % heads_per_part + 1) * dv]

    q = [rope(q_ref[:, h * dk:(h + 1) * dk].astype(F32)) for h in heads]
    k = [rope(k_ref[:, h * dk:(h + 1) * dk].astype(F32)) for h in heads]
    qb = [x.astype(BF16) for x in q]
    kb = [x.astype(BF16) for x in k]
    yield
    s = [_dot_nt(qb[h], kb[h]) for h in heads]
    yield
    sb = [(s[h] * decay_ref[h]).astype(BF16) for h in heads]
    q_in = [(q[h] * widen(qk_decay_ref[h, 0])).astype(BF16) for h in heads]
    k_in = [(k[h] * widen(qk_decay_ref[h, 1])).astype(BF16) for h in heads]
    st_b = [state_ref[h].astype(BF16) for h in heads]
    yield
    o = [_dot(sb[h], v_of(h)) + _dot(q_in[h], st_b[h]) for h in heads]
    incr = [_dot_tn(k_in[h], v_of(h)) for h in heads]
    yield
    for h in heads:
        state_ref[h] = math.exp(log_gamma[h] * tile) * state_ref[h] + incr[h]
        o_ref[:, h * dv:(h + 1) * dv] = o[h].astype(BF16)


def _mixers_kernel(pos0_ref, gq_ref, gk_ref, gv_ref, glr_ref, glr_next_ref, wdec_ref, bdec_ref,
                   rq_ref, rk_ref, *rest, gla_dk, gla_dv, ret_dk, ret_dv, n_parts):
    rv_refs = rest[:n_parts]
    (freq_ref, og_ref, or_ref, gla_state_ref, tri_ref, g_ref,
     ret_state_ref, decay_ref, qk_decay_ref, rot_ref) = rest[n_parts:]
    tile = tri_ref.shape[0]
    n_sub = gq_ref.shape[0] // tile
    t = pl.program_id(1)

    @pl.when(t == 0)
    def _():
        _gla_init(glr_ref.at[pl.ds(0, tile)], wdec_ref, bdec_ref, gla_state_ref, tri_ref, g_ref)
        _ret_init(freq_ref, ret_state_ref, decay_ref, qk_decay_ref, rot_ref, ret_dk)

    for sub in range(n_sub):
        rs = pl.ds(sub * tile, tile)
        slot = sub % 2 if n_sub % 2 == 0 else (t * n_sub + sub) % 2
        glr_next = (glr_ref.at[pl.ds((sub + 1) * tile, tile)] if sub + 1 < n_sub
                    else glr_next_ref)
        gla = _gla_stages(slot, 1 - slot, gq_ref.at[rs], gk_ref.at[rs], gv_ref.at[rs], glr_next,
                          wdec_ref, bdec_ref, og_ref.at[rs], gla_state_ref, tri_ref, g_ref,
                          gla_dk, gla_dv)
        ret = _ret_stages(pos0_ref[pl.program_id(0), t * n_sub + sub], rq_ref.at[rs],
                          rk_ref.at[rs], [r.at[rs] for r in rv_refs], freq_ref, or_ref.at[rs],
                          ret_state_ref, decay_ref, qk_decay_ref, rot_ref, ret_dk, ret_dv)
        for which in "rrgrrggrgg":
            next(ret if which == "r" else gla, None)


def _mixers(p3, glr3, wdec, bdec, pos0, inv_freq, cols, ret_dk, ret_dv, tile, tb):
    b, s, _ = p3.shape
    gla_dk = wdec.shape[1] // GLA_HEADS
    gla_dv = 2 * gla_dk
    gqk_w, gv_w = GLA_HEADS * gla_dk, GLA_HEADS * gla_dv
    rqk_w, rv_w = RET_HEADS * ret_dk, RET_HEADS * ret_dv
    part_w = math.gcd(cols["rv"], rv_w)
    n_parts = rv_w // part_w
    assert part_w % ret_dv == 0 and tb % tile == 0
    n_sub = tb // tile
    last = s // tile - 1

    def col_spec(off, w):
        blk = off // w
        return pl.BlockSpec((None, tb, w), lambda i, t, pos: (i, t, blk))

    grid_spec = pltpu.PrefetchScalarGridSpec(
        num_scalar_prefetch=1,
        grid=(b, s // tb),
        in_specs=[
            col_spec(cols["gq"], gqk_w),
            col_spec(cols["gk"], gqk_w),
            col_spec(cols["gv"], gv_w),
            pl.BlockSpec((None, tb, LANES), lambda i, t, pos: (i, t, 0)),
            pl.BlockSpec((None, tile, LANES),
                         lambda i, t, pos: (i, jnp.minimum((t + 1) * n_sub, last), 0)),
            pl.BlockSpec((LANES, gqk_w), lambda i, t, pos: (0, 0)),
            pl.BlockSpec((1, gqk_w), lambda i, t, pos: (0, 0)),
            col_spec(cols["rq"], rqk_w),
            col_spec(cols["rk"], rqk_w),
            *[col_spec(cols["rv"] + j * part_w, part_w) for j in range(n_parts)],
            pl.BlockSpec((1, ret_dk // 2), lambda i, t, pos: (0, 0)),
        ],
        out_specs=[
            pl.BlockSpec((None, tb, gv_w), lambda i, t, pos: (i, t, 0)),
            pl.BlockSpec((None, tb, rv_w), lambda i, t, pos: (i, t, 0)),
        ],
        scratch_shapes=[
            pltpu.VMEM((GLA_HEADS, gla_dk, gla_dv), F32),
            pltpu.VMEM((tile, tile), BF16),
            pltpu.VMEM((2, tile, gqk_w), F32),
            pltpu.VMEM((RET_HEADS, ret_dk, ret_dv), F32),
            pltpu.VMEM((RET_HEADS, tile, tile), F32),
            pltpu.VMEM((RET_HEADS, 2, tile, LANES), F32),
            pltpu.VMEM((2, tile, ret_dk // 2), F32),
        ],
    )
    return pl.pallas_call(
        functools.partial(_mixers_kernel, gla_dk=gla_dk, gla_dv=gla_dv, ret_dk=ret_dk,
                          ret_dv=ret_dv, n_parts=n_parts),
        grid_spec=grid_spec,
        out_shape=[
            jax.ShapeDtypeStruct((b, s, gv_w), BF16),
            jax.ShapeDtypeStruct((b, s, rv_w), BF16),
        ],
        compiler_params=pltpu.CompilerParams(
            dimension_semantics=("parallel", "arbitrary"),
            vmem_limit_bytes=VMEM_LIMIT),
        name="mixers",
    )(pos0, p3, p3, p3, glr3, glr3, wdec, bdec, p3, p3, *([p3] * n_parts), inv_freq)


def _mix_kernel(*refs, gla_dv, ret_dv, n_rg):
    og_ref, or_ref, gr_ref = refs[:3]
    rg_refs = refs[3:3 + n_rg]
    (ag_ref, ar_ref, x_ref, wgo_ref, wro_ref, wmo_ref,
     ln_post_ref, ln_pre_ref, x1_ref, h2_ref) = refs[3 + n_rg:]

    def project(o_ref, gate_refs, w_ref, dv, rs):
        heads_per_ref = gate_refs[0].shape[1] // dv
        y = None
        for h in range(o_ref.shape[1] // dv):
            gate = gate_refs[h // heads_per_ref][rs, (h % heads_per_ref) * dv:
                                                 (h % heads_per_ref + 1) * dv]
            o_n = _head_norm_gate(o_ref[rs, h * dv:(h + 1) * dv].astype(F32), gate)
            part = _dot(o_n, w_ref[h * dv:(h + 1) * dv, :])
            y = part if y is None else y + part
        return y

    tm = x_ref.shape[0]
    rs = slice(0, tm)
    y_gla = project(og_ref, (gr_ref,), wgo_ref, gla_dv, rs)
    y_ret = project(or_ref, rg_refs, wro_ref, ret_dv, rs)
    merged = (_sigmoid(ag_ref[...]) * y_gla.astype(BF16)
              + _sigmoid(ar_ref[...]) * y_ret.astype(BF16))
    rows = math.gcd(MIX_ROWS, tm)
    for r in range(tm // rows):
        rs = slice(r * rows, (r + 1) * rows)
        mo = _dot(merged[rs, :], wmo_ref[...])
        x1 = x_ref[rs, :] + _rms(mo) * ln_post_ref[...]
        x1_ref[rs, :] = x1
        h2_ref[rs, :] = (_rms(x1) * ln_pre_ref[...]).astype(BF16)


def _resident(shape):
    return pl.BlockSpec(shape, lambda i: (0,) * len(shape), pipeline_mode=pl.Buffered(1))


def _mix_out(og, orr, p, x2, wgo, wro, wmo, ln_post, ln_pre, cols, tm):
    t, d = x2.shape
    gla_v, ret_v = og.shape[1], orr.shape[1]
    rg_w = math.gcd(cols["rg"], ret_v)
    n_rg = ret_v // rg_w

    def col_spec(off, w):
        blk = off // w
        return pl.BlockSpec((tm, w), lambda i: (i, blk))

    return pl.pallas_call(
        functools.partial(_mix_kernel, gla_dv=gla_v // GLA_HEADS, ret_dv=ret_v // RET_HEADS,
                          n_rg=n_rg),
        grid=(t // tm,),
        in_specs=[
            pl.BlockSpec((tm, gla_v), lambda i: (i, 0)),
            pl.BlockSpec((tm, ret_v), lambda i: (i, 0)),
            col_spec(cols["gr"], gla_v),
            *[col_spec(cols["rg"] + j * rg_w, rg_w) for j in range(n_rg)],
            col_spec(cols["ag"], d),
            col_spec(cols["ar"], d),
            pl.BlockSpec((tm, d), lambda i: (i, 0)),
            _resident(wgo.shape),
            _resident(wro.shape),
            _resident(wmo.shape),
            _resident((1, d)),
            _resident((1, d)),
        ],
        out_specs=[
            pl.BlockSpec((tm, d), lambda i: (i, 0)),
            pl.BlockSpec((tm, d), lambda i: (i, 0)),
        ],
        out_shape=[
            jax.ShapeDtypeStruct((t, d), F32),
            jax.ShapeDtypeStruct((t, d), BF16),
        ],
        compiler_params=pltpu.CompilerParams(
            dimension_semantics=("parallel",),
            vmem_limit_bytes=VMEM_LIMIT),
        name="mix_out",
    )(og, orr, p, *([p] * n_rg), p, p, x2, wgo, wro, wmo, ln_post, ln_pre)


_GELU_C = math.sqrt(2.0 / math.pi)


def _ffn_kernel(h_ref, halo_ref, x1_ref, wup_ref, cw_ref, cb_ref, wdn_ref, ln_ref, o_ref,
                u0_ref, u1_ref, u2_ref, u3_ref, f_ref, z_ref, *, tiles_per_seq, d_ff, tn):
    u_ref = (u0_ref, u1_ref, u2_ref, u3_ref)
    tm, d = o_ref.shape
    n_phase = CONV_PHASES
    rows_pp = tm // n_phase
    first = (pl.program_id(0) % tiles_per_seq) == 0
    halo = jnp.where(first, jnp.zeros_like(halo_ref[...]), halo_ref[...])
    h_ext = jnp.concatenate([halo, h_ref[...]], axis=0)

    def up(c):
        for half in range(2):
            c0 = half * d_ff + c * tn
            u = _dot(h_ext, wup_ref[:, c0:c0 + tn])
            for l in range(tn // LANES):
                u_ref[2 * (c % 2) + half][l] = u[:, l * LANES:(l + 1) * LANES]

    def conv(slot, c0, gain):
        slabs = []
        for l in range(tn // LANES):
            cl = slice(c0 + l * LANES, c0 + (l + 1) * LANES)
            taps = [cw_ref[j:j + 1, cl] * gain for j in range(CONV_W)]
            bias = cb_ref[:, cl] * gain
            phases = []
            for s in range(n_phase):
                y = bias
                for j in range(CONV_W):
                    r0 = BF16_ROWS - (CONV_W - 1) + j + s
                    y = y + u_ref[slot][l, pl.ds(r0, rows_pp, stride=n_phase), :] * taps[j]
                phases.append(y)
            slabs.append(jnp.concatenate(phases, axis=0))
        return jnp.concatenate(slabs, axis=1)

    n_tiles = d_ff // tn
    group = -(-n_tiles // DOWN_GROUPS)
    parts, k0 = [], 0
    up(0)
    for c in range(n_tiles):
        if c + 1 < n_tiles:
            up(c + 1)
        half_val = conv(2 * (c % 2), c * tn, 0.5)
        g = conv(2 * (c % 2) + 1, d_ff + c * tn, 1.0)
        t = jnp.tanh(g * (_GELU_C + (_GELU_C * 0.044715) * (g * g)))
        f_ref[:, c * tn:(c + 1) * tn] = ((g + g * t) * half_val).astype(BF16)
        if (c + 1) % group == 0 or c + 1 == n_tiles:
            k1 = (c + 1) * tn
            parts.append(_dot(f_ref[:, k0:k1], wdn_ref[k0:k1, :]))
            k0 = k1

    z = _rms(functools.reduce(lambda a, b: a + b, parts)) * ln_ref[...]
    for l in range(d // LANES):
        for s in range(n_phase):
            z_ref[l, pl.ds(s, rows_pp, stride=n_phase), :] = (
                z[s * rows_pp:(s + 1) * rows_pp, l * LANES:(l + 1) * LANES])
    o_ref[...] = x1_ref[...] + jnp.concatenate([z_ref[l] for l in range(d // LANES)], axis=1)


def _conv_ffn(h2, x1, wup, cw, cb, wdn, ln, seq, tm, tn):
    t, d = x1.shape
    d_ff = wdn.shape[0]
    halo_blocks = tm // BF16_ROWS
    return pl.pallas_call(
        functools.partial(_ffn_kernel, tiles_per_seq=seq // tm, d_ff=d_ff, tn=tn),
        grid=(t // tm,),
        in_specs=[
            pl.BlockSpec((tm, d), lambda i: (i, 0)),
            pl.BlockSpec((BF16_ROWS, d), lambda i: (jnp.maximum(i * halo_blocks - 1, 0), 0)),
            pl.BlockSpec((tm, d), lambda i: (i, 0)),
            _resident(wup.shape),
            _resident(cw.shape),
            _resident(cb.shape),
            _resident(wdn.shape),
            _resident((1, d)),
        ],
        out_specs=pl.BlockSpec((tm, d), lambda i: (i, 0)),
        out_shape=jax.ShapeDtypeStruct((t, d), F32),
        scratch_shapes=[pltpu.VMEM((tn // LANES, BF16_ROWS + tm, LANES), F32)] * 4
        + [pltpu.VMEM((tm, d_ff), BF16), pltpu.VMEM((d // LANES, tm, LANES), F32)],
        compiler_params=pltpu.CompilerParams(
            dimension_semantics=("parallel",),
            vmem_limit_bytes=VMEM_LIMIT),
        name="conv_ffn",
    )(h2, h2, x1, wup, cw, cb, wdn, ln)


def _largest_tile(n, cap, quantum):
    best = quantum
    for cand in range(quantum, min(n, cap) + 1, quantum):
        if n % cand == 0:
            best = cand
    return best


def kernel(x, positions, ln_pre_mix, w_in, w_gla_decay, b_gla_decay, w_gla_out, w_ret_out,
           w_mix_out, ln_post_mix, ln_pre_ffn, w_ffn_up, conv_w, conv_b, w_ffn_down, ln_post_ffn):
    b, s, d = x.shape
    depth = w_in.shape[0]
    gla_qk = w_gla_decay.shape[2]
    gla_v = w_gla_out.shape[1]
    ret_v = w_ret_out.shape[1]
    ret_qk = ret_v // 2
    d_ff = w_ffn_down.shape[1]
    ret_dk, ret_dv = ret_qk // RET_HEADS, ret_v // RET_HEADS

    names = ("gq", "gk", "gv", "gr", "glr", "rq", "rk", "rv", "rg", "ag", "ar")
    widths = (gla_qk, gla_qk, gla_v, gla_v, GLA_GATE_RANK, ret_qk, ret_qk, ret_v, ret_v, d, d)
    lr0 = sum(widths[:names.index("glr")])
    cols, off = {}, 0
    for nm, w in zip(names, widths):
        if nm != "glr":
            cols[nm] = off
            off += w
    n_main = off

    inv_freq = (ROPE_BASE ** (-jnp.arange(0, ret_dk, 2, dtype=F32) / ret_dk))[None, :]

    tm_in = _largest_tile(b * s, 1024, 256)
    tn_in = _largest_tile(n_main, 2816, 256)
    tb = _largest_tile(s, 256, CHUNK)
    tb_mix = 2 * tb if s % (2 * tb) == 0 else tb
    tm_mix = _largest_tile(s, 512, 256)
    tm_ffn = _largest_tile(s, 512, 256)
    tn_ffn = _largest_tile(d_ff, 256, 256)
    pos0 = positions[:, ::tb]

    x2 = x.reshape(b * s, d)
    for layer in range(depth):
        w = w_in[layer]
        w_main = _drop_columns(w, lr0, GLA_GATE_RANK, _largest_tile(n_main, 1024, 256))
        w_lr = jnp.pad(w.T[lr0:lr0 + GLA_GATE_RANK],
                       ((0, LANES - GLA_GATE_RANK), (0, 0))).astype(BF16)
        wdec = jnp.pad(w_gla_decay[layer], ((0, LANES - GLA_GATE_RANK), (0, 0))).astype(BF16)
        bdec = b_gla_decay[layer][None, :]

        p, glr = _in_proj(x2, ln_pre_mix[layer][None, :], w_main, w_lr, tm_in, tn_in)
        p3 = p.reshape(b, s, -1)
        og, orr = _mixers(p3, glr.reshape(b, s, LANES), wdec, bdec, pos0, inv_freq, cols,
                          ret_dk, ret_dv, tb, tb_mix)
        x1, h2 = _mix_out(og.reshape(b * s, gla_v), orr.reshape(b * s, ret_v), p, x2,
                          w_gla_out[layer].astype(BF16), w_ret_out[layer].astype(BF16),
                          w_mix_out[layer].astype(BF16), ln_post_mix[layer][None, :],
                          ln_pre_ffn[layer][None, :], cols, tm_mix)
        x2 = _conv_ffn(h2, x1, w_ffn_up[layer].astype(BF16), conv_w[layer],
                       conv_b[layer][None, :], w_ffn_down[layer].astype(BF16),
                       ln_post_ffn[layer][None, :], s, tm_ffn, tn_ffn)
    return x2.reshape(b, s, d)
```

```python
import functools
import math

import jax
import jax.numpy as jnp
from jax import lax
from jax.experimental import pallas as pl
from jax.experimental.pallas import tpu as pltpu

F32 = jnp.float32
BF16 = jnp.bfloat16

CHUNK = 64
SUB = 16
EPS = 1e-6
GLA_HEADS = 4
GLA_GATE_RANK = 16
GLA_TAU = 16.0
LOG2_E = math.log2(math.e)
RET_HEADS = 4
ROPE_BASE = 10000.0
CONV_W = 3
LANES = 128
BF16_ROWS = 16
MIX_ROWS = 512
CONV_PHASES = 4
DOWN_GROUPS = 4

VMEM_LIMIT = 56 * 1024 * 1024


def _rms(x):
    return x * lax.rsqrt(jnp.mean(x * x, axis=-1, keepdims=True) + EPS)


def _sigmoid(x):
    return 0.5 + 0.5 * jnp.tanh(0.5 * x)


def _dot(a, b):
    return jnp.dot(a, b, preferred_element_type=F32)


def _dot_nt(a, b):
    return lax.dot_general(a, b, (((1,), (1,)), ((), ())), preferred_element_type=F32)


def _dot_tn(a, b):
    return lax.dot_general(a, b, (((0,), (0,)), ((), ())), preferred_element_type=F32)


def _head_norm_gate(o, gate):
    mu = jnp.mean(o, axis=-1, keepdims=True)
    d = o - mu
    var = jnp.mean(d * d, axis=-1, keepdims=True)
    h = 0.5 * gate
    return (d * lax.rsqrt(var + EPS)).astype(BF16) * (h + h * jnp.tanh(h))


def _in_proj_kernel(x_ref, lnw_ref, w_ref, wlr_ref, p_ref, glr_ref, *, tn):
    h = (_rms(x_ref[...]) * lnw_ref[...]).astype(BF16)
    glr_ref[...] = _dot_nt(h, wlr_ref[...]).astype(BF16)
    for c in range(w_ref.shape[1] // tn):
        cs = slice(c * tn, (c + 1) * tn)
        p_ref[:, cs] = _dot(h, w_ref[:, cs]).astype(BF16)


def _regroup_kernel(wt_ref, o_ref):
    o_ref[...] = wt_ref[...].T.astype(BF16)


def _drop_columns(w, start, gap, tw):
    k, n = w.shape
    n_out = n - gap
    assert start % tw == 0 and n_out % tw == 0 and gap % BF16_ROWS == 0
    first_shifted = start // tw
    return pl.pallas_call(
        _regroup_kernel,
        grid=(n_out // tw,),
        in_specs=[pl.BlockSpec(
            (pl.Element(tw), pl.Element(k)),
            lambda j: (pl.multiple_of(j * tw + jnp.where(j >= first_shifted, gap, 0), BF16_ROWS),
                       0))],
        out_specs=pl.BlockSpec((k, tw), lambda j: (0, j)),
        out_shape=jax.ShapeDtypeStruct((k, n_out), BF16),
        compiler_params=pltpu.CompilerParams(
            dimension_semantics=("parallel",),
            vmem_limit_bytes=VMEM_LIMIT),
        name="w_in_regroup",
    )(w.T)


def _in_proj(x2, lnw, w_main, w_lr, tm, tn):
    t, d = x2.shape
    nc = w_main.shape[1]
    return pl.pallas_call(
        functools.partial(_in_proj_kernel, tn=tn),
        grid=(t // tm,),
        in_specs=[
            pl.BlockSpec((tm, d), lambda i: (i, 0)),
            _resident((1, d)),
            _resident((d, nc)),
            _resident((LANES, d)),
        ],
        out_specs=[
            pl.BlockSpec((tm, nc), lambda i: (i, 0)),
            pl.BlockSpec((tm, LANES), lambda i: (i, 0)),
        ],
        out_shape=[
            jax.ShapeDtypeStruct((t, nc), BF16),
            jax.ShapeDtypeStruct((t, LANES), BF16),
        ],
        compiler_params=pltpu.CompilerParams(
            dimension_semantics=("parallel",),
            vmem_limit_bytes=VMEM_LIMIT),
        name="in_proj",
    )(x2, lnw, w_main, w_lr)


def _chunk_cumsum(x, tri):
    hi = x.astype(BF16)
    r1 = x - hi.astype(F32)
    mid = r1.astype(BF16)
    lo = (r1 - mid.astype(F32)).astype(BF16)
    return _dot(tri, hi) + _dot(tri, mid) + _dot(tri, lo)


def _pad_rows(a, before, total):
    parts = []
    if before:
        parts.append(jnp.zeros((before, a.shape[1]), a.dtype))
    parts.append(a)
    after = total - before - a.shape[0]
    if after:
        parts.append(jnp.zeros((after, a.shape[1]), a.dtype))
    return jnp.concatenate(parts, axis=0) if len(parts) > 1 else a


def _log_decay_prefix(glr, wdec_ref, bdec_ref, tri_ref):
    z = _dot(glr, wdec_ref[...]) + bdec_ref[...]
    log2_a = (jnp.minimum(z, 0.0) * LOG2_E
              - jnp.log2(1.0 + jnp.exp2(jnp.abs(z) * -LOG2_E))) * (1.0 / GLA_TAU)
    return _chunk_cumsum(log2_a, tri_ref[...])


def _gla_init(glr0_ref, wdec_ref, bdec_ref, state_ref, tri_ref, g_ref):
    tb = tri_ref.shape[0]
    state_ref[...] = jnp.zeros_like(state_ref)
    r = lax.broadcasted_iota(jnp.int32, (tb, tb), 0)
    c = lax.broadcasted_iota(jnp.int32, (tb, tb), 1)
    tri_ref[...] = jnp.where((c <= r) & (c // CHUNK == r // CHUNK), 1.0, 0.0).astype(BF16)
    g_ref[0] = _log_decay_prefix(glr0_ref[...], wdec_ref, bdec_ref, tri_ref)


def _gla_stages(slot, next_slot, q_ref, k_ref, v_ref, glr_next_ref, wdec_ref, bdec_ref, o_ref,
                state_ref, tri_ref, g_ref, dk, dv):
    tb = q_ref.shape[0]
    nsub = CHUNK // SUB
    scale = dk ** -0.5
    row = lax.broadcasted_iota(jnp.int32, (CHUNK, CHUNK), 0)
    col = lax.broadcasted_iota(jnp.int32, (CHUNK, CHUNK), 1)
    lower = row >= col
    n_chunks = tb // CHUNK
    units = [(h, c) for c in range(n_chunks) for h in range(GLA_HEADS)]

    def rows(c):
        return slice(c * CHUNK, (c + 1) * CHUNK)

    def lanes(h, w):
        return slice(h * w, (h + 1) * w)

    prep = {}
    for h, c in units:
        g = g_ref[slot, rows(c), lanes(h, dk)]
        q = q_ref[rows(c), lanes(h, dk)].astype(F32) * scale
        k = k_ref[rows(c), lanes(h, dk)].astype(F32)
        refs = [g[i * SUB:i * SUB + 1, :] for i in range(nsub)]
        e = jnp.exp2(g - jnp.concatenate(
            [jnp.broadcast_to(rf, (SUB, dk)) for rf in refs], axis=0))
        qe = (q * e).astype(BF16)
        ke = (k * e).astype(BF16)
        q_lo, k_lo, q_up, k_up = [], [], [], []
        for i in range(nsub):
            n = (i + 1) * SUB
            f = jnp.exp2(refs[i] - g[:n, :])
            kf = k[:n] * f
            k_lo.append(_pad_rows(kf.astype(BF16), 0, CHUNK))
            q_up.append(_pad_rows((q[:n] * f).astype(BF16), 0, CHUNK))
            q_lo.append(_pad_rows(qe[i * SUB:n], i * SUB, CHUNK))
            k_up.append(_pad_rows(ke[i * SUB:n], i * SUB, CHUNK))
        g_last = g[CHUNK - 1:CHUNK, :]
        prep[h, c] = dict(
            q_lo=jnp.concatenate(q_lo, axis=1), k_lo=jnp.concatenate(k_lo, axis=1),
            q_up=jnp.concatenate(q_up, axis=1), k_up=jnp.concatenate(k_up, axis=1),
            q_in=(q * jnp.exp2(g)).astype(BF16),
            k_dec=(kf * jnp.exp2(g_last - refs[nsub - 1])).astype(BF16),
            decay=jnp.exp2(g_last))
    yield

    scores, incr = {}, {}
    for h, c in units:
        p = prep[h, c]
        s_lo = _dot_nt(p["q_lo"], p["k_lo"])
        s_up = _dot_nt(p["q_up"], p["k_up"])
        scores[h, c] = jnp.where(lower, s_lo, s_up).astype(BF16)
        incr[h, c] = _dot_tn(p["k_dec"], v_ref[rows(c), lanes(h, dv)])
    yield

    g_ref[next_slot] = _log_decay_prefix(glr_next_ref[...], wdec_ref, bdec_ref, tri_ref)
    yield

    state_in = {}
    for h in range(GLA_HEADS):
        st = state_ref[h]
        for c in range(n_chunks):
            state_in[h, c] = st.astype(BF16)
            col_decay = jnp.transpose(jnp.broadcast_to(prep[h, c]["decay"], (dk, dk)))
            st = st * jnp.concatenate([col_decay] * (dv // dk), axis=1) + incr[h, c]
        state_ref[h] = st
    yield

    for h, c in units:
        o = _dot(jnp.concatenate([prep[h, c]["q_in"], scores[h, c]], axis=1),
                 jnp.concatenate([state_in[h, c], v_ref[rows(c), lanes(h, dv)]], axis=0))
        o_ref[rows(c), lanes(h, dv)] = o.astype(BF16)


def _ret_init(freq_ref, state_ref, decay_ref, qk_decay_ref, rot_ref, dk):
    tile = decay_ref.shape[1]
    state_ref[...] = jnp.zeros_like(state_ref)
    row = lax.broadcasted_iota(jnp.int32, (tile, tile), 0)
    col = lax.broadcasted_iota(jnp.int32, (tile, tile), 1)
    visible = (col // CHUNK) <= (row // CHUNK)
    dist = jnp.abs(row - col).astype(F32)
    idx = lax.broadcasted_iota(jnp.int32, (tile, LANES), 0).astype(F32)
    for h in range(RET_HEADS):
        log_gamma = math.log(1.0 - 2.0 ** (-5.0 - h))
        decay_ref[h] = jnp.where(visible, jnp.exp(log_gamma * dist), 0.0) * (dk ** -0.5)
        qk_decay_ref[h, 0] = jnp.exp(log_gamma * (idx + 1.0))
        qk_decay_ref[h, 1] = jnp.exp(log_gamma * (tile - 1.0 - idx)) * (dk ** -0.5)
    ang = lax.broadcasted_iota(jnp.int32, (tile, dk // 2), 0).astype(F32) * freq_ref[...]
    rot_ref[0] = jnp.cos(ang)
    rot_ref[1] = jnp.sin(ang)


def _ret_stages(pos0, q_ref, k_ref, v_refs, freq_ref, o_ref,
                state_ref, decay_ref, qk_decay_ref, rot_ref, dk, dv):
    tile = q_ref.shape[0]
    half = dk // 2
    heads_per_part = RET_HEADS // len(v_refs)

    ang0 = pos0.astype(F32) * freq_ref[...]
    c0, s0 = jnp.cos(ang0), jnp.sin(ang0)
    cr, sr = rot_ref[0], rot_ref[1]
    cos = c0 * cr - s0 * sr
    sin = s0 * cr + c0 * sr

    def rope(x):
        x1, x2 = x[:, :half], x[:, half:]
        return jnp.concatenate([x1 * cos - x2 * sin, x1 * sin + x2 * cos], axis=1)

    def widen(tbl):
        return jnp.concatenate([tbl] * (dk // LANES), axis=1)

    heads = range(RET_HEADS)
    log_gamma = [math.log(1.0 - 2.0 ** (-5.0 - h)) for h in heads]

    def v_of(h):
        return v_refs[h // heads_per_part][:, (h % heads_per_part) * dv:
                                           (h % heads_per_part + 1) * dv]

    q = [rope(q_ref[:, h * dk:(h + 1) * dk].astype(F32)) for h in heads]
    k = [rope(k_ref[:, h * dk:(h + 1) * dk].astype(F32)) for h in heads]
    qb = [x.astype(BF16) for x in q]
    kb = [x.astype(BF16) for x in k]
    yield
    s = [_dot_nt(qb[h], kb[h]) for h in heads]
    yield
    sb = [(s[h] * decay_ref[h]).astype(BF16) for h in heads]
    q_in = [(q[h] * widen(qk_decay_ref[h, 0])).astype(BF16) for h in heads]
    k_in = [(k[h] * widen(qk_decay_ref[h, 1])).astype(BF16) for h in heads]
    st_b = [state_ref[h].astype(BF16) for h in heads]
    yield
    o = [_dot(sb[h], v_of(h)) + _dot(q_in[h], st_b[h]) for h in heads]
    incr = [_dot_tn(k_in[h], v_of(h)) for h in heads]
    yield
    for h in heads:
        state_ref[h] = math.exp(log_gamma[h] * tile) * state_ref[h] + incr[h]
        o_ref[:, h * dv:(h + 1) * dv] = o[h].astype(BF16)


def _mixers_kernel(pos0_ref, gq_ref, gk_ref, gv_ref, glr_ref, glr_next_ref, wdec_ref, bdec_ref,
                   rq_ref, rk_ref, *rest, gla_dk, gla_dv, ret_dk, ret_dv, n_parts):
    rv_refs = rest[:n_parts]
    (freq_ref, og_ref, or_ref, gla_state_ref, tri_ref, g_ref,
     ret_state_ref, decay_ref, qk_decay_ref, rot_ref) = rest[n_parts:]
    tile = tri_ref.shape[0]
    n_sub = gq_ref.shape[0] // tile
    t = pl.program_id(1)

    @pl.when(t == 0)
    def _():
        _gla_init(glr_ref.at[pl.ds(0, tile)], wdec_ref, bdec_ref, gla_state_ref, tri_ref, g_ref)
        _ret_init(freq_ref, ret_state_ref, decay_ref, qk_decay_ref, rot_ref, ret_dk)

    for sub in range(n_sub):
        rs = pl.ds(sub * tile, tile)
        slot = sub % 2 if n_sub % 2 == 0 else (t * n_sub + sub) % 2
        glr_next = (glr_ref.at[pl.ds((sub + 1) * tile, tile)] if sub + 1 < n_sub
                    else glr_next_ref)
        gla = _gla_stages(slot, 1 - slot, gq_ref.at[rs], gk_ref.at[rs], gv_ref.at[rs], glr_next,
                          wdec_ref, bdec_ref, og_ref.at[rs], gla_state_ref, tri_ref, g_ref,
                          gla_dk, gla_dv)
        ret = _ret_stages(pos0_ref[pl.program_id(0), t * n_sub + sub], rq_ref.at[rs],
                          rk_ref.at[rs], [r.at[rs] for r in rv_refs], freq_ref, or_ref.at[rs],
                          ret_state_ref, decay_ref, qk_decay_ref, rot_ref, ret_dk, ret_dv)
        for which in "rrgrrggrgg":
            next(ret if which == "r" else gla, None)


def _mixers(p3, glr3, wdec, bdec, pos0, inv_freq, cols, ret_dk, ret_dv, tile, tb):
    b, s, _ = p3.shape
    gla_dk = wdec.shape[1] // GLA_HEADS
    gla_dv = 2 * gla_dk
    gqk_w, gv_w = GLA_HEADS * gla_dk, GLA_HEADS * gla_dv
    rqk_w, rv_w = RET_HEADS * ret_dk, RET_HEADS * ret_dv
    part_w = math.gcd(cols["rv"], rv_w)
    n_parts = rv_w // part_w
    assert part_w % ret_dv == 0 and tb % tile == 0
    n_sub = tb // tile
    last = s // tile - 1

    def col_spec(off, w):
        blk = off // w
        return pl.BlockSpec((None, tb, w), lambda i, t, pos: (i, t, blk))

    grid_spec = pltpu.PrefetchScalarGridSpec(
        num_scalar_prefetch=1,
        grid=(b, s // tb),
        in_specs=[
            col_spec(cols["gq"], gqk_w),
            col_spec(cols["gk"], gqk_w),
            col_spec(cols["gv"], gv_w),
            pl.BlockSpec((None, tb, LANES), lambda i, t, pos: (i, t, 0)),
            pl.BlockSpec((None, tile, LANES),
                         lambda i, t, pos: (i, jnp.minimum((t + 1) * n_sub, last), 0)),
            pl.BlockSpec((LANES, gqk_w), lambda i, t, pos: (0, 0)),
            pl.BlockSpec((1, gqk_w), lambda i, t, pos: (0, 0)),
            col_spec(cols["rq"], rqk_w),
            col_spec(cols["rk"], rqk_w),
            *[col_spec(cols["rv"] + j * part_w, part_w) for j in range(n_parts)],
            pl.BlockSpec((1, ret_dk // 2), lambda i, t, pos: (0, 0)),
        ],
        out_specs=[
            pl.BlockSpec((None, tb, gv_w), lambda i, t, pos: (i, t, 0)),
            pl.BlockSpec((None, tb, rv_w), lambda i, t, pos: (i, t, 0)),
        ],
        scratch_shapes=[
            pltpu.VMEM((GLA_HEADS, gla_dk, gla_dv), F32),
            pltpu.VMEM((tile, tile), BF16),
            pltpu.VMEM((2, tile, gqk_w), F32),
            pltpu.VMEM((RET_HEADS, ret_dk, ret_dv), F32),
            pltpu.VMEM((RET_HEADS, tile, tile), F32),
            pltpu.VMEM((RET_HEADS, 2, tile, LANES), F32),
            pltpu.VMEM((2, tile, ret_dk // 2), F32),
        ],
    )
    return pl.pallas_call(
        functools.partial(_mixers_kernel, gla_dk=gla_dk, gla_dv=gla_dv, ret_dk=ret_dk,
                          ret_dv=ret_dv, n_parts=n_parts),
        grid_spec=grid_spec,
        out_shape=[
            jax.ShapeDtypeStruct((b, s, gv_w), BF16),
            jax.ShapeDtypeStruct((b, s, rv_w), BF16),
        ],
        compiler_params=pltpu.CompilerParams(
            dimension_semantics=("parallel", "arbitrary"),
            vmem_limit_bytes=VMEM_LIMIT),
        name="mixers",
    )(pos0, p3, p3, p3, glr3, glr3, wdec, bdec, p3, p3, *([p3] * n_parts), inv_freq)


def _mix_kernel(*refs, gla_dv, ret_dv, n_rg):
    og_ref, or_ref, gr_ref = refs[:3]
    rg_refs = refs[3:3 + n_rg]
    (ag_ref, ar_ref, x_ref, wgo_ref, wro_ref, wmo_ref,
     ln_post_ref, ln_pre_ref, x1_ref, h2_ref) = refs[3 + n_rg:]

    def project(o_ref, gate_refs, w_ref, dv, rs):
        heads_per_ref = gate_refs[0].shape[1] // dv
        y = None
        for h in range(o_ref.shape[1] // dv):
            gate = gate_refs[h // heads_per_ref][rs, (h % heads_per_ref) * dv:
                                                 (h % heads_per_ref + 1) * dv]
            o_n = _head_norm_gate(o_ref[rs, h * dv:(h + 1) * dv].astype(F32), gate)
            part = _dot(o_n, w_ref[h * dv:(h + 1) * dv, :])
            y = part if y is None else y + part
        return y

    tm = x_ref.shape[0]
    rs = slice(0, tm)
    y_gla = project(og_ref, (gr_ref,), wgo_ref, gla_dv, rs)
    y_ret = project(or_ref, rg_refs, wro_ref, ret_dv, rs)
    merged = (_sigmoid(ag_ref[...]) * y_gla.astype(BF16)
              + _sigmoid(ar_ref[...]) * y_ret.astype(BF16))
    rows = math.gcd(MIX_ROWS, tm)
    for r in range(tm // rows):
        rs = slice(r * rows, (r + 1) * rows)
        mo = _dot(merged[rs, :], wmo_ref[...])
        x1 = x_ref[rs, :] + _rms(mo) * ln_post_ref[...]
        x1_ref[rs, :] = x1
        h2_ref[rs, :] = (_rms(x1) * ln_pre_ref[...]).astype(BF16)


def _resident(shape):
    return pl.BlockSpec(shape, lambda i: (0,) * len(shape), pipeline_mode=pl.Buffered(1))


def _mix_out(og, orr, p, x2, wgo, wro, wmo, ln_post, ln_pre, cols, tm):
    t, d = x2.shape
    gla_v, ret_v = og.shape[1], orr.shape[1]
    rg_w = math.gcd(cols["rg"], ret_v)
    n_rg = ret_v // rg_w

    def col_spec(off, w):
        blk = off // w
        return pl.BlockSpec((tm, w), lambda i: (i, blk))

    return pl.pallas_call(
        functools.partial(_mix_kernel, gla_dv=gla_v // GLA_HEADS, ret_dv=ret_v // RET_HEADS,
                          n_rg=n_rg),
        grid=(t // tm,),
        in_specs=[
            pl.BlockSpec((tm, gla_v), lambda i: (i, 0)),
            pl.BlockSpec((tm, ret_v), lambda i: (i, 0)),
            col_spec(cols["gr"], gla_v),
            *[col_spec(cols["rg"] + j * rg_w, rg_w) for j in range(n_rg)],
            col_spec(cols["ag"], d),
            col_spec(cols["ar"], d),
            pl.BlockSpec((tm, d), lambda i: (i, 0)),
            _resident(wgo.shape),
            _resident(wro.shape),
            _resident(wmo.shape),
            _resident((1, d)),
            _resident((1, d)),
        ],
        out_specs=[
            pl.BlockSpec((tm, d), lambda i: (i, 0)),
            pl.BlockSpec((tm, d), lambda i: (i, 0)),
        ],
        out_shape=[
            jax.ShapeDtypeStruct((t, d), F32),
            jax.ShapeDtypeStruct((t, d), BF16),
        ],
        compiler_params=pltpu.CompilerParams(
            dimension_semantics=("parallel",),
            vmem_limit_bytes=VMEM_LIMIT),
        name="mix_out",
    )(og, orr, p, *([p] * n_rg), p, p, x2, wgo, wro, wmo, ln_post, ln_pre)


_GELU_C = math.sqrt(2.0 / math.pi)


def _ffn_kernel(h_ref, halo_ref, x1_ref, wup_ref, cw_ref, cb_ref, wdn_ref, ln_ref, o_ref,
                u0_ref, u1_ref, u2_ref, u3_ref, f_ref, z_ref, *, tiles_per_seq, d_ff, tn):
    u_ref = (u0_ref, u1_ref, u2_ref, u3_ref)
    tm, d = o_ref.shape
    n_phase = CONV_PHASES
    rows_pp = tm // n_phase
    first = (pl.program_id(0) % tiles_per_seq) == 0
    halo = jnp.where(first, jnp.zeros_like(halo_ref[...]), halo_ref[...])
    h_ext = jnp.concatenate([halo, h_ref[...]], axis=0)

    def up(c):
        for half in range(2):
            c0 = half * d_ff + c * tn
            u = _dot(h_ext, wup_ref[:, c0:c0 + tn])
            for l in range(tn // LANES):
                u_ref[2 * (c % 2) + half][l] = u[:, l * LANES:(l + 1) * LANES]

    def conv(slot, c0, gain):
        slabs = []
        for l in range(tn // LANES):
            cl = slice(c0 + l * LANES, c0 + (l + 1) * LANES)
            taps = [cw_ref[j:j + 1, cl] * gain for j in range(CONV_W)]
            bias = cb_ref[:, cl] * gain
            phases = []
            for s in range(n_phase):
                y = bias
                for j in range(CONV_W):
                    r0 = BF16_ROWS - (CONV_W - 1) + j + s
                    y = y + u_ref[slot][l, pl.ds(r0, rows_pp, stride=n_phase), :] * taps[j]
                phases.append(y)
            slabs.append(jnp.concatenate(phases, axis=0))
        return jnp.concatenate(slabs, axis=1)

    n_tiles = d_ff // tn
    group = -(-n_tiles // DOWN_GROUPS)
    parts, k0 = [], 0
    up(0)
    for c in range(n_tiles):
        if c + 1 < n_tiles:
            up(c + 1)
        half_val = conv(2 * (c % 2), c * tn, 0.5)
        g = conv(2 * (c % 2) + 1, d_ff + c * tn, 1.0)
        t = jnp.tanh(g * (_GELU_C + (_GELU_C * 0.044715) * (g * g)))
        f_ref[:, c * tn:(c + 1) * tn] = ((g + g * t) * half_val).astype(BF16)
        if (c + 1) % group == 0 or c + 1 == n_tiles:
            k1 = (c + 1) * tn
            parts.append(_dot(f_ref[:, k0:k1], wdn_ref[k0:k1, :]))
            k0 = k1

    z = _rms(functools.reduce(lambda a, b: a + b, parts)) * ln_ref[...]
    for l in range(d // LANES):
        for s in range(n_phase):
            z_ref[l, pl.ds(s, rows_pp, stride=n_phase), :] = (
                z[s * rows_pp:(s + 1) * rows_pp, l * LANES:(l + 1) * LANES])
    o_ref[...] = x1_ref[...] + jnp.concatenate([z_ref[l] for l in range(d // LANES)], axis=1)


def _conv_ffn(h2, x1, wup, cw, cb, wdn, ln, seq, tm, tn):
    t, d = x1.shape
    d_ff = wdn.shape[0]
    halo_blocks = tm // BF16_ROWS
    return pl.pallas_call(
        functools.partial(_ffn_kernel, tiles_per_seq=seq // tm, d_ff=d_ff, tn=tn),
        grid=(t // tm,),
        in_specs=[
            pl.BlockSpec((tm, d), lambda i: (i, 0)),
            pl.BlockSpec((BF16_ROWS, d), lambda i: (jnp.maximum(i * halo_blocks - 1, 0), 0)),
            pl.BlockSpec((tm, d), lambda i: (i, 0)),
            _resident(wup.shape),
            _resident(cw.shape),
            _resident(cb.shape),
            _resident(wdn.shape),
            _resident((1, d)),
        ],
        out_specs=pl.BlockSpec((tm, d), lambda i: (i, 0)),
        out_shape=jax.ShapeDtypeStruct((t, d), F32),
        scratch_shapes=[pltpu.VMEM((tn // LANES, BF16_ROWS + tm, LANES), F32)] * 4
        + [pltpu.VMEM((tm, d_ff), BF16), pltpu.VMEM((d // LANES, tm, LANES), F32)],
        compiler_params=pltpu.CompilerParams(
            dimension_semantics=("parallel",),
            vmem_limit_bytes=VMEM_LIMIT),
        name="conv_ffn",
    )(h2, h2, x1, wup, cw, cb, wdn, ln)


def _largest_tile(n, cap, quantum):
    best = quantum
    for cand in range(quantum, min(n, cap) + 1, quantum):
        if n % cand == 0:
            best = cand
    return best


def kernel(x, positions, ln_pre_mix, w_in, w_gla_decay, b_gla_decay, w_gla_out, w_ret_out,
           w_mix_out, ln_post_mix, ln_pre_ffn, w_ffn_up, conv_w, conv_b, w_ffn_down, ln_post_ffn):
    b, s, d = x.shape
    depth = w_in.shape[0]
    gla_qk = w_gla_decay.shape[2]
    gla_v = w_gla_out.shape[1]
    ret_v = w_ret_out.shape[1]
    ret_qk = ret_v // 2
    d_ff = w_ffn_down.shape[1]
    ret_dk, ret_dv = ret_qk // RET_HEADS, ret_v // RET_HEADS

    names = ("gq", "gk", "gv", "gr", "glr", "rq", "rk", "rv", "rg", "ag", "ar")
    widths = (gla_qk, gla_qk, gla_v, gla_v, GLA_GATE_RANK, ret_qk, ret_qk, ret_v, ret_v, d, d)
    lr0 = sum(widths[:names.index("glr")])
    cols, off = {}, 0
    for nm, w in zip(names, widths):
        if nm != "glr":
            cols[nm] = off
            off += w
    n_main = off

    inv_freq = (ROPE_BASE ** (-jnp.arange(0, ret_dk, 2, dtype=F32) / ret_dk))[None, :]

    tm_in = _largest_tile(b * s, 512, 256)
    tn_in = _largest_tile(n_main, 1024, 256)
    tb = _largest_tile(s, 256, CHUNK)
    tb_mix = 2 * tb if s % (2 * tb) == 0 else tb
    tm_mix = _largest_tile(s, 512, 256)
    tm_ffn = _largest_tile(s, 512, 256)
    tn_ffn = _largest_tile(d_ff, 256, 256)
    pos0 = positions[:, ::tb]

    x2 = x.reshape(b * s, d)
    for layer in range(depth):
        w = w_in[layer]
        w_main = _drop_columns(w, lr0, GLA_GATE_RANK, _largest_tile(n_main, 1024, 256))
        w_lr = jnp.pad(w.T[lr0:lr0 + GLA_GATE_RANK],
                       ((0, LANES - GLA_GATE_RANK), (0, 0))).astype(BF16)
        wdec = jnp.pad(w_gla_decay[layer], ((0, LANES - GLA_GATE_RANK), (0, 0))).astype(BF16)
        bdec = b_gla_decay[layer][None, :]

        p, glr = _in_proj(x2, ln_pre_mix[layer][None, :], w_main, w_lr, tm_in, tn_in)
        p3 = p.reshape(b, s, -1)
        og, orr = _mixers(p3, glr.reshape(b, s, LANES), wdec, bdec, pos0, inv_freq, cols,
                          ret_dk, ret_dv, tb, tb_mix)
        x1, h2 = _mix_out(og.reshape(b * s, gla_v), orr.reshape(b * s, ret_v), p, x2,
                          w_gla_out[layer].astype(BF16), w_ret_out[layer].astype(BF16),
                          w_mix_out[layer].astype(BF16), ln_post_mix[layer][None, :],
                          ln_pre_ffn[layer][None, :], cols, tm_mix)
        x2 = _conv_ffn(h2, x1, w_ffn_up[layer].astype(BF16), conv_w[layer],
                       conv_b[layer][None, :], w_ffn_down[layer].astype(BF16),
                       ln_post_ffn[layer][None, :], s, tm_ffn, tn_ffn)
    return x2.reshape(b, s, d)
```

```python
import functools
import math

import jax
import jax.numpy as jnp
from jax import lax
from jax.experimental import pallas as pl
from jax.experimental.pallas import tpu as pltpu

F32 = jnp.float32
BF16 = jnp.bfloat16

CHUNK = 64
SUB = 16
EPS = 1e-6
GLA_HEADS = 4
GLA_GATE_RANK = 16
GLA_TAU = 16.0
LOG2_E = math.log2(math.e)
RET_HEADS = 4
ROPE_BASE = 10000.0
CONV_W = 3
LANES = 128
BF16_ROWS = 16
MIX_ROWS = 512
CONV_PHASES = 4
DOWN_GROUPS = 4

VMEM_LIMIT = 56 * 1024 * 1024


def _rms(x):
    return x * lax.rsqrt(jnp.mean(x * x, axis=-1, keepdims=True) + EPS)


def _sigmoid(x):
    return 0.5 + 0.5 * jnp.tanh(0.5 * x)


def _dot(a, b):
    return jnp.dot(a, b, preferred_element_type=F32)


def _dot_nt(a, b):
    return lax.dot_general(a, b, (((1,), (1,)), ((), ())), preferred_element_type=F32)


def _dot_tn(a, b):
    return lax.dot_general(a, b, (((0,), (0,)), ((), ())), preferred_element_type=F32)


def _head_norm_gate(o, gate):
    mu = jnp.mean(o, axis=-1, keepdims=True)
    d = o - mu
    var = jnp.mean(d * d, axis=-1, keepdims=True)
    h = 0.5 * gate
    return (d * lax.rsqrt(var + EPS)).astype(BF16) * (h + h * jnp.tanh(h))


def _in_proj_kernel(x_ref, lnw_ref, w_ref, wlr_ref, p_ref, glr_ref, *, tn):
    h = (_rms(x_ref[...]) * lnw_ref[...]).astype(BF16)
    glr_ref[...] = _dot_nt(h, wlr_ref[...]).astype(BF16)
    for c in range(w_ref.shape[1] // tn):
        cs = slice(c * tn, (c + 1) * tn)
        p_ref[:, cs] = _dot(h, w_ref[:, cs]).astype(BF16)


def _regroup_kernel(wt_ref, o_ref):
    o_ref[...] = wt_ref[...].T.astype(BF16)


def _drop_columns(w, start, gap, tw):
    k, n = w.shape
    n_out = n - gap
    assert start % tw == 0 and n_out % tw == 0 and gap % BF16_ROWS == 0
    first_shifted = start // tw
    return pl.pallas_call(
        _regroup_kernel,
        grid=(n_out // tw,),
        in_specs=[pl.BlockSpec(
            (pl.Element(tw), pl.Element(k)),
            lambda j: (pl.multiple_of(j * tw + jnp.where(j >= first_shifted, gap, 0), BF16_ROWS),
                       0))],
        out_specs=pl.BlockSpec((k, tw), lambda j: (0, j)),
        out_shape=jax.ShapeDtypeStruct((k, n_out), BF16),
        compiler_params=pltpu.CompilerParams(
            dimension_semantics=("parallel",),
            vmem_limit_bytes=VMEM_LIMIT),
        name="w_in_regroup",
    )(w.T)


def _in_proj(x2, lnw, w_main, w_lr, tm, tn):
    t, d = x2.shape
    nc = w_main.shape[1]
    return pl.pallas_call(
        functools.partial(_in_proj_kernel, tn=tn),
        grid=(t // tm,),
        in_specs=[
            pl.BlockSpec((tm, d), lambda i: (i, 0)),
            _resident((1, d)),
            _resident((d, nc)),
            _resident((LANES, d)),
        ],
        out_specs=[
            pl.BlockSpec((tm, nc), lambda i: (i, 0)),
            pl.BlockSpec((tm, LANES), lambda i: (i, 0)),
        ],
        out_shape=[
            jax.ShapeDtypeStruct((t, nc), BF16),
            jax.ShapeDtypeStruct((t, LANES), BF16),
        ],
        compiler_params=pltpu.CompilerParams(
            dimension_semantics=("parallel",),
            vmem_limit_bytes=VMEM_LIMIT),
        name="in_proj",
    )(x2, lnw, w_main, w_lr)


def _chunk_cumsum(x, tri):
    hi = x.astype(BF16)
    r1 = x - hi.astype(F32)
    mid = r1.astype(BF16)
    lo = (r1 - mid.astype(F32)).astype(BF16)
    return _dot(tri, hi) + _dot(tri, mid) + _dot(tri, lo)


def _pad_rows(a, before, total):
    parts = []
    if before:
        parts.append(jnp.zeros((before, a.shape[1]), a.dtype))
    parts.append(a)
    after = total - before - a.shape[0]
    if after:
        parts.append(jnp.zeros((after, a.shape[1]), a.dtype))
    return jnp.concatenate(parts, axis=0) if len(parts) > 1 else a


def _log_decay_prefix(glr, wdec_ref, bdec_ref, tri_ref):
    z = _dot(glr, wdec_ref[...]) + bdec_ref[...]
    log2_a = (jnp.minimum(z, 0.0) * LOG2_E
              - jnp.log2(1.0 + jnp.exp2(jnp.abs(z) * -LOG2_E))) * (1.0 / GLA_TAU)
    return _chunk_cumsum(log2_a, tri_ref[...])


def _gla_init(glr0_ref, wdec_ref, bdec_ref, state_ref, tri_ref, g_ref):
    tb = tri_ref.shape[0]
    state_ref[...] = jnp.zeros_like(state_ref)
    r = lax.broadcasted_iota(jnp.int32, (tb, tb), 0)
    c = lax.broadcasted_iota(jnp.int32, (tb, tb), 1)
    tri_ref[...] = jnp.where((c <= r) & (c // CHUNK == r // CHUNK), 1.0, 0.0).astype(BF16)
    g_ref[0] = _log_decay_prefix(glr0_ref[...], wdec_ref, bdec_ref, tri_ref)


def _gla_stages(slot, next_slot, q_ref, k_ref, v_ref, glr_next_ref, wdec_ref, bdec_ref, o_ref,
                state_ref, tri_ref, g_ref, dk, dv):
    tb = q_ref.shape[0]
    nsub = CHUNK // SUB
    scale = dk ** -0.5
    row = lax.broadcasted_iota(jnp.int32, (CHUNK, CHUNK), 0)
    col = lax.broadcasted_iota(jnp.int32, (CHUNK, CHUNK), 1)
    lower = row >= col
    n_chunks = tb // CHUNK
    units = [(h, c) for c in range(n_chunks) for h in range(GLA_HEADS)]

    def rows(c):
        return slice(c * CHUNK, (c + 1) * CHUNK)

    def lanes(h, w):
        return slice(h * w, (h + 1) * w)

    prep = {}
    for h, c in units:
        g = g_ref[slot, rows(c), lanes(h, dk)]
        q = q_ref[rows(c), lanes(h, dk)].astype(F32) * scale
        k = k_ref[rows(c), lanes(h, dk)].astype(F32)
        refs = [g[i * SUB:i * SUB + 1, :] for i in range(nsub)]
        e = jnp.exp2(g - jnp.concatenate(
            [jnp.broadcast_to(rf, (SUB, dk)) for rf in refs], axis=0))
        qe = (q * e).astype(BF16)
        ke = (k * e).astype(BF16)
        q_lo, k_lo, q_up, k_up = [], [], [], []
        for i in range(nsub):
            n = (i + 1) * SUB
            f = jnp.exp2(refs[i] - g[:n, :])
            kf = k[:n] * f
            k_lo.append(_pad_rows(kf.astype(BF16), 0, CHUNK))
            q_up.append(_pad_rows((q[:n] * f).astype(BF16), 0, CHUNK))
            q_lo.append(_pad_rows(qe[i * SUB:n], i * SUB, CHUNK))
            k_up.append(_pad_rows(ke[i * SUB:n], i * SUB, CHUNK))
        g_last = g[CHUNK - 1:CHUNK, :]
        prep[h, c] = dict(
            q_lo=jnp.concatenate(q_lo, axis=1), k_lo=jnp.concatenate(k_lo, axis=1),
            q_up=jnp.concatenate(q_up, axis=1), k_up=jnp.concatenate(k_up, axis=1),
            q_in=(q * jnp.exp2(g)).astype(BF16),
            k_dec=(kf * jnp.exp2(g_last - refs[nsub - 1])).astype(BF16),
            decay=jnp.exp2(g_last))
    yield

    scores, incr = {}, {}
    for h, c in units:
        p = prep[h, c]
        s_lo = _dot_nt(p["q_lo"], p["k_lo"])
        s_up = _dot_nt(p["q_up"], p["k_up"])
        scores[h, c] = jnp.where(lower, s_lo, s_up).astype(BF16)
        incr[h, c] = _dot_tn(p["k_dec"], v_ref[rows(c), lanes(h, dv)])
    yield

    g_ref[next_slot] = _log_decay_prefix(glr_next_ref[...], wdec_ref, bdec_ref, tri_ref)
    yield

    state_in = {}
    for h in range(GLA_HEADS):
        st = state_ref[h]
        for c in range(n_chunks):
            state_in[h, c] = st.astype(BF16)
            col_decay = jnp.transpose(jnp.broadcast_to(prep[h, c]["decay"], (dk, dk)))
            st = st * jnp.concatenate([col_decay] * (dv // dk), axis=1) + incr[h, c]
        state_ref[h] = st
    yield

    for h, c in units:
        o = _dot(jnp.concatenate([prep[h, c]["q_in"], scores[h, c]], axis=1),
                 jnp.concatenate([state_in[h, c], v_ref[rows(c), lanes(h, dv)]], axis=0))
        o_ref[rows(c), lanes(h, dv)] = o.astype(BF16)


def _ret_init(freq_ref, state_ref, decay_ref, qk_decay_ref, rot_ref, dk):
    tile = decay_ref.shape[1]
    state_ref[...] = jnp.zeros_like(state_ref)
    row = lax.broadcasted_iota(jnp.int32, (tile, tile), 0)
    col = lax.broadcasted_iota(jnp.int32, (tile, tile), 1)
    visible = (col // CHUNK) <= (row // CHUNK)
    dist = jnp.abs(row - col).astype(F32)
    idx = lax.broadcasted_iota(jnp.int32, (tile, LANES), 0).astype(F32)
    for h in range(RET_HEADS):
        log_gamma = math.log(1.0 - 2.0 ** (-5.0 - h))
        decay_ref[h] = jnp.where(visible, jnp.exp(log_gamma * dist), 0.0) * (dk ** -0.5)
        qk_decay_ref[h, 0] = jnp.exp(log_gamma * (idx + 1.0))
        qk_decay_ref[h, 1] = jnp.exp(log_gamma * (tile - 1.0 - idx)) * (dk ** -0.5)
    ang = lax.broadcasted_iota(jnp.int32, (tile, dk // 2), 0).astype(F32) * freq_ref[...]
    rot_ref[0] = jnp.cos(ang)
    rot_ref[1] = jnp.sin(ang)


def _ret_stages(pos0, q_ref, k_ref, v_refs, freq_ref, o_ref,
                state_ref, decay_ref, qk_decay_ref, rot_ref, dk, dv):
    tile = q_ref.shape[0]
    half = dk // 2
    heads_per_part = RET_HEADS // len(v_refs)

    ang0 = pos0.astype(F32) * freq_ref[...]
    c0, s0 = jnp.cos(ang0), jnp.sin(ang0)
    cr, sr = rot_ref[0], rot_ref[1]
    cos = c0 * cr - s0 * sr
    sin = s0 * cr + c0 * sr

    def rope(x):
        x1, x2 = x[:, :half], x[:, half:]
        return jnp.concatenate([x1 * cos - x2 * sin, x1 * sin + x2 * cos], axis=1)

    def widen(tbl):
        return jnp.concatenate([tbl] * (dk // LANES), axis=1)

    heads = range(RET_HEADS)
    log_gamma = [math.log(1.0 - 2.0 ** (-5.0 - h)) for h in heads]

    def v_of(h):
        return v_refs[h // heads_per_part][:, (h % heads_per_part) * dv:
                                           (h % heads_per_part + 1) * dv]

    q = [rope(q_ref[:, h * dk:(h + 1) * dk].astype(F32)) for h in heads]
    k = [rope(k_ref[:, h * dk:(h + 1) * dk].astype(F32)) for h in heads]
    qb = [x.astype(BF16) for x in q]
    kb = [x.astype(BF16) for x in k]
    yield
    s = [_dot_nt(qb[h], kb[h]) for h in heads]
    yield
    sb = [(s[h] * decay_ref[h]).astype(BF16) for h in heads]
    q_in = [(q[h] * widen(qk_decay_ref[h, 0])).astype(BF16) for h in heads]
    k_in = [(k[h] * widen(qk_decay_ref[h, 1])).astype(BF16) for h in heads]
    st_b = [state_ref[h].astype(BF16) for h in heads]
    yield
    o = [_dot(sb[h], v_of(h)) + _dot(q_in[h], st_b[h]) for h in heads]
    incr = [_dot_tn(k_in[h], v_of(h)) for h in heads]
    yield
    for h in heads:
        state_ref[h] = math.exp(log_gamma[h] * tile) * state_ref[h] + incr[h]
        o_ref[:, h * dv:(h + 1) * dv] = o[h].astype(BF16)


def _mixers_kernel(pos0_ref, gq_ref, gk_ref, gv_ref, glr_ref, glr_next_ref, wdec_ref, bdec_ref,
                   rq_ref, rk_ref, *rest, gla_dk, gla_dv, ret_dk, ret_dv, n_parts):
    rv_refs = rest[:n_parts]
    (freq_ref, og_ref, or_ref, gla_state_ref, tri_ref, g_ref,
     ret_state_ref, decay_ref, qk_decay_ref, rot_ref) = rest[n_parts:]
    tile = tri_ref.shape[0]
    n_sub = gq_ref.shape[0] // tile
    t = pl.program_id(1)

    @pl.when(t == 0)
    def _():
        _gla_init(glr_ref.at[pl.ds(0, tile)], wdec_ref, bdec_ref, gla_state_ref, tri_ref, g_ref)
        _ret_init(freq_ref, ret_state_ref, decay_ref, qk_decay_ref, rot_ref, ret_dk)

    for sub in range(n_sub):
        rs = pl.ds(sub * tile, tile)
        slot = sub % 2 if n_sub % 2 == 0 else (t * n_sub + sub) % 2
        glr_next = (glr_ref.at[pl.ds((sub + 1) * tile, tile)] if sub + 1 < n_sub
                    else glr_next_ref)
        gla = _gla_stages(slot, 1 - slot, gq_ref.at[rs], gk_ref.at[rs], gv_ref.at[rs], glr_next,
                          wdec_ref, bdec_ref, og_ref.at[rs], gla_state_ref, tri_ref, g_ref,
                          gla_dk, gla_dv)
        ret = _ret_stages(pos0_ref[pl.program_id(0), t * n_sub + sub], rq_ref.at[rs],
                          rk_ref.at[rs], [r.at[rs] for r in rv_refs], freq_ref, or_ref.at[rs],
                          ret_state_ref, decay_ref, qk_decay_ref, rot_ref, ret_dk, ret_dv)
        for which in "rrgrrggrgg":
            next(ret if which == "r" else gla, None)


def _mixers(p3, glr3, wdec, bdec, pos0, inv_freq, cols, ret_dk, ret_dv, tile, tb):
    b, s, _ = p3.shape
    gla_dk = wdec.shape[1] // GLA_HEADS
    gla_dv = 2 * gla_dk
    gqk_w, gv_w = GLA_HEADS * gla_dk, GLA_HEADS * gla_dv
    rqk_w, rv_w = RET_HEADS * ret_dk, RET_HEADS * ret_dv
    part_w = math.gcd(cols["rv"], rv_w)
    n_parts = rv_w // part_w
    assert part_w % ret_dv == 0 and tb % tile == 0
    n_sub = tb // tile
    last = s // tile - 1

    def col_spec(off, w):
        blk = off // w
        return pl.BlockSpec((None, tb, w), lambda i, t, pos: (i, t, blk))

    grid_spec = pltpu.PrefetchScalarGridSpec(
        num_scalar_prefetch=1,
        grid=(b, s // tb),
        in_specs=[
            col_spec(cols["gq"], gqk_w),
            col_spec(cols["gk"], gqk_w),
            col_spec(cols["gv"], gv_w),
            pl.BlockSpec((None, tb, LANES), lambda i, t, pos: (i, t, 0)),
            pl.BlockSpec((None, tile, LANES),
                         lambda i, t, pos: (i, jnp.minimum((t + 1) * n_sub, last), 0)),
            pl.BlockSpec((LANES, gqk_w), lambda i, t, pos: (0, 0)),
            pl.BlockSpec((1, gqk_w), lambda i, t, pos: (0, 0)),
            col_spec(cols["rq"], rqk_w),
            col_spec(cols["rk"], rqk_w),
            *[col_spec(cols["rv"] + j * part_w, part_w) for j in range(n_parts)],
            pl.BlockSpec((1, ret_dk // 2), lambda i, t, pos: (0, 0)),
        ],
        out_specs=[
            pl.BlockSpec((None, tb, gv_w), lambda i, t, pos: (i, t, 0)),
            pl.BlockSpec((None, tb, rv_w), lambda i, t, pos: (i, t, 0)),
        ],
        scratch_shapes=[
            pltpu.VMEM((GLA_HEADS, gla_dk, gla_dv), F32),
            pltpu.VMEM((tile, tile), BF16),
            pltpu.VMEM((2, tile, gqk_w), F32),
            pltpu.VMEM((RET_HEADS, ret_dk, ret_dv), F32),
            pltpu.VMEM((RET_HEADS, tile, tile), F32),
            pltpu.VMEM((RET_HEADS, 2, tile, LANES), F32),
            pltpu.VMEM((2, tile, ret_dk // 2), F32),
        ],
    )
    return pl.pallas_call(
        functools.partial(_mixers_kernel, gla_dk=gla_dk, gla_dv=gla_dv, ret_dk=ret_dk,
                          ret_dv=ret_dv, n_parts=n_parts),
        grid_spec=grid_spec,
        out_shape=[
            jax.ShapeDtypeStruct((b, s, gv_w), BF16),
            jax.ShapeDtypeStruct((b, s, rv_w), BF16),
        ],
        compiler_params=pltpu.CompilerParams(
            dimension_semantics=("parallel", "arbitrary"),
            vmem_limit_bytes=VMEM_LIMIT),
        name="mixers",
    )(pos0, p3, p3, p3, glr3, glr3, wdec, bdec, p3, p3, *([p3] * n_parts), inv_freq)


def _mix_kernel(*refs, gla_dv, ret_dv, n_rg):
    og_ref, or_ref, gr_ref = refs[:3]
    rg_refs = refs[3:3 + n_rg]
    (ag_ref, ar_ref, x_ref, wgo_ref, wro_ref, wmo_ref,
     ln_post_ref, ln_pre_ref, x1_ref, h2_ref) = refs[3 + n_rg:]

    def project(o_ref, gate_refs, w_ref, dv, rs):
        heads_per_ref = gate_refs[0].shape[1] // dv
        y = None
        for h in range(o_ref.shape[1] // dv):
            gate = gate_refs[h // heads_per_ref][rs, (h % heads_per_ref) * dv:
                                                 (h % heads_per_ref + 1) * dv]
            o_n = _head_norm_gate(o_ref[rs, h * dv:(h + 1) * dv].astype(F32), gate)
            part = _dot(o_n, w_ref[h * dv:(h + 1) * dv, :])
            y = part if y is None else y + part
        return y

    tm = x_ref.shape[0]
    rs = slice(0, tm)
    y_gla = project(og_ref, (gr_ref,), wgo_ref, gla_dv, rs)
    y_ret = project(or_ref, rg_refs, wro_ref, ret_dv, rs)
    merged = (_sigmoid(ag_ref[...]) * y_gla.astype(BF16)
              + _sigmoid(ar_ref[...]) * y_ret.astype(BF16))
    rows = math.gcd(MIX_ROWS, tm)
    for r in range(tm // rows):
        rs = slice(r * rows, (r + 1) * rows)
        mo = _dot(merged[rs, :], wmo_ref[...])
        x1 = x_ref[rs, :] + _rms(mo) * ln_post_ref[...]
        x1_ref[rs, :] = x1
        h2_ref[rs, :] = (_rms(x1) * ln_pre_ref[...]).astype(BF16)


def _resident(shape):
    return pl.BlockSpec(shape, lambda i: (0,) * len(shape), pipeline_mode=pl.Buffered(1))


def _mix_out(og, orr, p, x2, wgo, wro, wmo, ln_post, ln_pre, cols, tm):
    t, d = x2.shape
    gla_v, ret_v = og.shape[1], orr.shape[1]
    rg_w = math.gcd(cols["rg"], ret_v)
    n_rg = ret_v // rg_w

    def col_spec(off, w):
        blk = off // w
        return pl.BlockSpec((tm, w), lambda i: (i, blk))

    return pl.pallas_call(
        functools.partial(_mix_kernel, gla_dv=gla_v // GLA_HEADS, ret_dv=ret_v // RET_HEADS,
                          n_rg=n_rg),
        grid=(t // tm,),
        in_specs=[
            pl.BlockSpec((tm, gla_v), lambda i: (i, 0)),
            pl.BlockSpec((tm, ret_v), lambda i: (i, 0)),
            col_spec(cols["gr"], gla_v),
            *[col_spec(cols["rg"] + j * rg_w, rg_w) for j in range(n_rg)],
            col_spec(cols["ag"], d),
            col_spec(cols["ar"], d),
            pl.BlockSpec((tm, d), lambda i: (i, 0)),
            _resident(wgo.shape),
            _resident(wro.shape),
            _resident(wmo.shape),
            _resident((1, d)),
            _resident((1, d)),
        ],
        out_specs=[
            pl.BlockSpec((tm, d), lambda i: (i, 0)),
            pl.BlockSpec((tm, d), lambda i: (i, 0)),
        ],
        out_shape=[
            jax.ShapeDtypeStruct((t, d), F32),
            jax.ShapeDtypeStruct((t, d), BF16),
        ],
        compiler_params=pltpu.CompilerParams(
            dimension_semantics=("parallel",),
            vmem_limit_bytes=VMEM_LIMIT),
        name="mix_out",
    )(og, orr, p, *([p] * n_rg), p, p, x2, wgo, wro, wmo, ln_post, ln_pre)


_GELU_C = math.sqrt(2.0 / math.pi)


def _ffn_kernel(h_ref, halo_ref, x1_ref, wup_ref, cw_ref, cb_ref, wdn_ref, ln_ref, o_ref,
                u0_ref, u1_ref, u2_ref, u3_ref, f_ref, *, tiles_per_seq, d_ff, tn):
    u_ref = (u0_ref, u1_ref, u2_ref, u3_ref)
    tm, d = o_ref.shape
    n_phase = CONV_PHASES
    rows_pp = tm // n_phase
    first = (pl.program_id(0) % tiles_per_seq) == 0
    halo = jnp.where(first, jnp.zeros_like(halo_ref[...]), halo_ref[...])
    h_ext = jnp.concatenate([halo, h_ref[...]], axis=0)

    def up(c):
        for half in range(2):
            c0 = half * d_ff + c * tn
            u = _dot(h_ext, wup_ref[:, c0:c0 + tn])
            for l in range(tn // LANES):
                u_ref[2 * (c % 2) + half][l] = u[:, l * LANES:(l + 1) * LANES]

    def conv(slot, c0, gain):
        slabs = []
        for l in range(tn // LANES):
            cl = slice(c0 + l * LANES, c0 + (l + 1) * LANES)
            taps = [cw_ref[j:j + 1, cl] * gain for j in range(CONV_W)]
            bias = cb_ref[:, cl] * gain
            phases = []
            for s in range(n_phase):
                y = bias
                for j in range(CONV_W):
                    r0 = BF16_ROWS - (CONV_W - 1) + j + s
                    y = y + u_ref[slot][l, pl.ds(r0, rows_pp, stride=n_phase), :] * taps[j]
                phases.append(y)
            slabs.append(jnp.concatenate(phases, axis=0))
        return jnp.concatenate(slabs, axis=1)

    n_tiles = d_ff // tn
    group = -(-n_tiles // DOWN_GROUPS)
    parts, k0 = [], 0
    up(0)
    for c in range(n_tiles):
        if c + 1 < n_tiles:
            up(c + 1)
        half_val = conv(2 * (c % 2), c * tn, 0.5)
        g = conv(2 * (c % 2) + 1, d_ff + c * tn, 1.0)
        t = jnp.tanh(g * (_GELU_C + (_GELU_C * 0.044715) * (g * g)))
        f = (g + g * t) * half_val
        for l in range(tn // LANES):
            for s in range(n_phase):
                f_ref[c * (tn // LANES) + l, pl.ds(s, rows_pp, stride=n_phase), :] = (
                    f[s * rows_pp:(s + 1) * rows_pp, l * LANES:(l + 1) * LANES])
        if (c + 1) % group == 0 or c + 1 == n_tiles:
            k1 = (c + 1) * tn
            lhs = jnp.concatenate([f_ref[j] for j in range(k0 // LANES, k1 // LANES)], axis=1)
            parts.append(_dot(lhs.astype(BF16), wdn_ref[k0:k1, :]))
            k0 = k1

    z = _rms(functools.reduce(lambda a, b: a + b, parts)) * ln_ref[...]
    o_ref[...] = x1_ref[...] + z


def _conv_ffn(h2, x1, wup, cw, cb, wdn, ln, seq, tm, tn):
    t, d = x1.shape
    d_ff = wdn.shape[0]
    halo_blocks = tm // BF16_ROWS
    return pl.pallas_call(
        functools.partial(_ffn_kernel, tiles_per_seq=seq // tm, d_ff=d_ff, tn=tn),
        grid=(t // tm,),
        in_specs=[
            pl.BlockSpec((tm, d), lambda i: (i, 0)),
            pl.BlockSpec((BF16_ROWS, d), lambda i: (jnp.maximum(i * halo_blocks - 1, 0), 0)),
            pl.BlockSpec((tm, d), lambda i: (i, 0)),
            _resident(wup.shape),
            _resident(cw.shape),
            _resident(cb.shape),
            _resident(wdn.shape),
            _resident((1, d)),
        ],
        out_specs=pl.BlockSpec((tm, d), lambda i: (i, 0)),
        out_shape=jax.ShapeDtypeStruct((t, d), F32),
        scratch_shapes=[pltpu.VMEM((tn // LANES, BF16_ROWS + tm, LANES), F32)] * 4
        + [pltpu.VMEM((d_ff // LANES, tm, LANES), F32)],
        compiler_params=pltpu.CompilerParams(
            dimension_semantics=("parallel",),
            vmem_limit_bytes=VMEM_LIMIT),
        name="conv_ffn",
    )(h2, h2, x1, wup, cw, cb, wdn, ln)


def _largest_tile(n, cap, quantum):
    best = quantum
    for cand in range(quantum, min(n, cap) + 1, quantum):
        if n % cand == 0:
            best = cand
    return best


def kernel(x, positions, ln_pre_mix, w_in, w_gla_decay, b_gla_decay, w_gla_out, w_ret_out,
           w_mix_out, ln_post_mix, ln_pre_ffn, w_ffn_up, conv_w, conv_b, w_ffn_down, ln_post_ffn):
    b, s, d = x.shape
    depth = w_in.shape[0]
    gla_qk = w_gla_decay.shape[2]
    gla_v = w_gla_out.shape[1]
    ret_v = w_ret_out.shape[1]
    ret_qk = ret_v // 2
    d_ff = w_ffn_down.shape[1]
    ret_dk, ret_dv = ret_qk // RET_HEADS, ret_v // RET_HEADS

    names = ("gq", "gk", "gv", "gr", "glr", "rq", "rk", "rv", "rg", "ag", "ar")
    widths = (gla_qk, gla_qk, gla_v, gla_v, GLA_GATE_RANK, ret_qk, ret_qk, ret_v, ret_v, d, d)
    lr0 = sum(widths[:names.index("glr")])
    cols, off = {}, 0
    for nm, w in zip(names, widths):
        if nm != "glr":
            cols[nm] = off
            off += w
    n_main = off

    inv_freq = (ROPE_BASE ** (-jnp.arange(0, ret_dk, 2, dtype=F32) / ret_dk))[None, :]

    tm_in = _largest_tile(b * s, 512, 256)
    tn_in = _largest_tile(n_main, 1024, 256)
    tb = _largest_tile(s, 256, CHUNK)
    tb_mix = 2 * tb if s % (2 * tb) == 0 else tb
    tm_mix = _largest_tile(s, 512, 256)
    tm_ffn = _largest_tile(s, 512, 256)
    tn_ffn = _largest_tile(d_ff, 256, 256)
    pos0 = positions[:, ::tb]

    x2 = x.reshape(b * s, d)
    for layer in range(depth):
        w = w_in[layer]
        w_main = _drop_columns(w, lr0, GLA_GATE_RANK, _largest_tile(n_main, 1024, 256))
        w_lr = jnp.pad(w.T[lr0:lr0 + GLA_GATE_RANK],
                       ((0, LANES - GLA_GATE_RANK), (0, 0))).astype(BF16)
        wdec = jnp.pad(w_gla_decay[layer], ((0, LANES - GLA_GATE_RANK), (0, 0))).astype(BF16)
        bdec = b_gla_decay[layer][None, :]

        p, glr = _in_proj(x2, ln_pre_mix[layer][None, :], w_main, w_lr, tm_in, tn_in)
        p3 = p.reshape(b, s, -1)
        og, orr = _mixers(p3, glr.reshape(b, s, LANES), wdec, bdec, pos0, inv_freq, cols,
                          ret_dk, ret_dv, tb, tb_mix)
        x1, h2 = _mix_out(og.reshape(b * s, gla_v), orr.reshape(b * s, ret_v), p, x2,
                          w_gla_out[layer].astype(BF16), w_ret_out[layer].astype(BF16),
                          w_mix_out[layer].astype(BF16), ln_post_mix[layer][None, :],
                          ln_pre_ffn[layer][None, :], cols, tm_mix)
        x2 = _conv_ffn(h2, x1, w_ffn_up[layer].astype(BF16), conv_w[layer],
                       conv_b[layer][None, :], w_ffn_down[layer].astype(BF16),
                       ln_post_ffn[layer][None, :], s, tm_ffn, tn_ffn)
    return x2.reshape(b, s, d)
```
